```python
import jax
import jax.numpy as jnp
from jax import lax
import numpy as np

D_MODEL = 2048
BATCH = 2
SEQ = 4096
DEPTH = 1
DEC_BATCH = 32
DEC_SEQ = 1
PAST_LEN = 16384
PAGE_SIZE = 128

HEAD_DIM = 128
N_HEADS_NSA = D_MODEL // (2 * HEAD_DIM)
N_KV_NSA = 2
GROUP_R = N_HEADS_NSA // N_KV_NSA
N_HEADS_SB = D_MODEL // (2 * HEAD_DIM)
L_CMP = 32
STRIDE = 16
SEL_BLOCK = 64
N_SELECT = 16
WINDOW = 512
CMP_HIDDEN = 2 * HEAD_DIM
D_FF = -(-8 * D_MODEL // (3 * 256)) * 256
QBLK = 128
ROPE_THETA = 10000.0
EPS = 1e-6
SCALE = HEAD_DIM ** -0.5
Q_NSA = N_HEADS_NSA * HEAD_DIM
KV_NSA = N_KV_NSA * HEAD_DIM
G_NSA = N_HEADS_NSA * 3
QKV_SB = N_HEADS_SB * HEAD_DIM
D_IN = Q_NSA + 6 * KV_NSA + G_NSA + 3 * QKV_SB + 2 * D_MODEL

kernel_name = 'nsa_stickbreaking_hybrid_step'


def _split_points():
    sizes = [Q_NSA] + [KV_NSA] * 6 + [G_NSA] + [QKV_SB] * 3 + [D_MODEL, D_MODEL]
    return [int(v) for v in np.cumsum(sizes)[:-1]]


def _rmsnorm(x, g):
    xf = x.astype(jnp.float32)
    y = xf * lax.rsqrt(jnp.mean(xf * xf, axis=-1, keepdims=True) + EPS)
    return y.astype(x.dtype) * g


def _heads(x, nh):
    return x.reshape(x.shape[0], x.shape[1], nh, HEAD_DIM)


def _rope(x, pos):
    half = HEAD_DIM // 2
    inv = ROPE_THETA ** (-2.0 * jnp.arange(half, dtype=jnp.float32) / HEAD_DIM)
    ang = pos.astype(jnp.float32)[:, None] * inv[None, :]
    cos = jnp.cos(ang)[None, :, None, :].astype(x.dtype)
    sin = jnp.sin(ang)[None, :, None, :].astype(x.dtype)
    x1, x2 = x[..., :half], x[..., half:]
    return jnp.concatenate([x1 * cos - x2 * sin, x1 * sin + x2 * cos], axis=-1)


def _masked_softmax(s, valid):
    s = jnp.where(valid, s, -jnp.inf)
    m = jnp.max(s, axis=-1, keepdims=True)
    m = jnp.where(jnp.isfinite(m), m, 0.0)
    p = jnp.exp(s - m)
    return p / jnp.maximum(jnp.sum(p, axis=-1, keepdims=True), 1e-30)


def _adaln(c, w_ada, b_ada):
    mod = (c @ w_ada + b_ada)[:, None, :]
    return jnp.split(mod, 6, axis=-1)


def _compress(x, pe, w1, b1, w2):
    n, t = x.shape[0], x.shape[1]
    nseg = t // STRIDE
    segs = x[:, :nseg * STRIDE].reshape(n, nseg, STRIDE, N_KV_NSA, HEAD_DIM)
    segs = segs.transpose(0, 1, 3, 2, 4).reshape(n, nseg, N_KV_NSA, STRIDE * HEAD_DIM)
    half = STRIDE * HEAD_DIM
    first = segs @ w1[:half]
    second = segs @ w1[half:]
    bias = pe.reshape(-1) @ w1 + b1
    h = jax.nn.gelu(first[:, :-1] + second[:, 1:] + bias)
    return h @ w2


def _compressed_kv(k_cmp, v_cmp, pe_ck, w_ck1, b_ck1, w_ck2, pe_cv, w_cv1, b_cv1, w_cv2):
    kc = _compress(k_cmp, pe_ck, w_ck1, b_ck1, w_ck2)
    c_start = jnp.arange(kc.shape[1]) * STRIDE
    kc = _rope(kc, c_start)
    vc = _compress(v_cmp, pe_cv, w_cv1, b_cv1, w_cv2)
    return kc, vc, c_start + (L_CMP - 1)


def _to_blocks(x):
    n, t = x.shape[0], x.shape[1]
    nb = -(-t // SEL_BLOCK)
    x = jnp.pad(x, ((0, 0), (0, nb * SEL_BLOCK - t), (0, 0), (0, 0)))
    return x.reshape(n, nb, SEL_BLOCK, N_KV_NSA, HEAD_DIM).transpose(0, 3, 1, 2, 4)


def _nsa_compressed(q, qpos, kc, vc, c_end):
    s = jnp.einsum('ntgrd,ncgd->ntgrc', q, kc).astype(jnp.float32) * SCALE
    valid = (c_end[None, :] <= qpos[:, None])[None, :, None, None, :]
    p = _masked_softmax(s, valid)
    o = jnp.einsum('ntgrc,ncgd->ntgrd', p.astype(vc.dtype), vc)
    return o, p


def _nsa_selected(q, qpos, p_cmp, kb, vb):
    n, n_sel = kb.shape[0], kb.shape[2]
    n_c = p_cmp.shape[-1]
    seg_per_blk = SEL_BLOCK // STRIDE
    nseg = n_sel * seg_per_blk
    imp = jnp.sum(p_cmp, axis=3)
    lead = ((0, 0), (0, 0), (0, 0))
    seg = jnp.pad(imp, lead + ((0, nseg - n_c),)) + jnp.pad(imp, lead + ((1, nseg - n_c - 1),))
    score = seg.reshape(imp.shape[0], imp.shape[1], imp.shape[2], n_sel, seg_per_blk).sum(-1)
    blk = jnp.arange(n_sel)[None, :]
    cur = (qpos // SEL_BLOCK)[:, None]
    forced = (blk == 0) | (blk == cur) | (blk == cur - 1)
    future = blk > cur
    score = jnp.where(future[None, :, None, :], -jnp.inf,
                      jnp.where(forced[None, :, None, :], jnp.inf, score))
    _, idx = lax.top_k(score, min(N_SELECT, n_sel))
    idx_t = idx.transpose(0, 2, 1, 3)
    ni = jnp.arange(n)[:, None, None, None]
    gi = jnp.arange(N_KV_NSA)[None, :, None, None]
    ks = kb[ni, gi, idx_t]
    vs = vb[ni, gi, idx_t]
    s = jnp.einsum('ntgrd,ngtkod->ntgrko', q, ks).astype(jnp.float32) * SCALE
    kpos = idx[..., None] * SEL_BLOCK + jnp.arange(SEL_BLOCK)
    valid = (kpos <= qpos[None, :, None, None, None])[:, :, :, None]
    sh = s.shape
    p = _masked_softmax(s.reshape(sh[0], sh[1], sh[2], sh[3], -1),
                        valid.reshape(sh[0], sh[1], sh[2], 1, -1)).reshape(sh)
    return jnp.einsum('ntgrko,ngtkod->ntgrd', p.astype(vs.dtype), vs)


def _nsa_window(q, qpos, kw, vw, kwpos):
    s = jnp.einsum('ntgrd,nsgd->ntgrs', q, kw).astype(jnp.float32) * SCALE
    dist = qpos[:, None] - kwpos[None, :]
    valid = ((dist >= 0) & (dist < WINDOW) & (kwpos[None, :] >= 0))[None, :, None, None, :]
    p = _masked_softmax(s, valid)
    return jnp.einsum('ntgrs,nsgd->ntgrd', p.astype(vw.dtype), vw)


def _stick_breaking(q, qpos, k_list, v_list, kpos):
    z = jnp.concatenate([jnp.einsum('nthd,nshd->nhts', q, k) for k in k_list], axis=-1)
    z = z.astype(jnp.float32) * SCALE
    causal = (kpos[None, :] < qpos[:, None])[None, None]
    log_1m = jnp.where(causal, jax.nn.log_sigmoid(-z), 0.0)
    rest = lax.cumsum(log_1m, axis=3, reverse=True) - log_1m
    a = jnp.where(causal, jnp.exp(jax.nn.log_sigmoid(z) + rest), 0.0)
    out = None
    off = 0
    for v in v_list:
        tk = v.shape[1]
        part = jnp.einsum('nhts,nshd->nthd', a[..., off:off + tk].astype(v.dtype), v)
        out = part if out is None else out + part
        off += tk
    return out


def _token_mixers(q_nsa, g_nsa, qpos, kc, vc, c_end, kb, vb, kw, vw, kwpos,
                  q_sb, k_sb_list, v_sb_list, kpos_sb):
    n, t = q_nsa.shape[0], q_nsa.shape[1]
    o_c, p_c = _nsa_compressed(q_nsa, qpos, kc, vc, c_end)
    o_s = _nsa_selected(q_nsa, qpos, p_c, kb, vb)
    o_w = _nsa_window(q_nsa, qpos, kw, vw, kwpos)
    g = jax.nn.sigmoid(g_nsa).reshape(n, t, N_KV_NSA, GROUP_R, 3)
    o_nsa = g[..., 0:1] * o_c + g[..., 1:2] * o_s + g[..., 2:3] * o_w
    o_sb = _stick_breaking(q_sb, qpos, k_sb_list, v_sb_list, kpos_sb)
    return o_nsa.reshape(n, t, Q_NSA), o_sb.reshape(n, t, QKV_SB)


def _project(x, c, pos, w_ada, b_ada, g_pre_mix, w_in):
    shift1, scale1, gate1, shift2, scale2, gate2 = _adaln(c, w_ada, b_ada)
    h = _rmsnorm(x, g_pre_mix) * (1 + scale1) + shift1
    (q_nsa, k_cmp, v_cmp, k_sel, v_sel, k_win, v_win, g_nsa,
     q_sb, k_sb, v_sb, m_nsa, m_sb) = jnp.split(h @ w_in, _split_points(), axis=-1)
    n, t = x.shape[0], x.shape[1]
    q_nsa = _rope(_heads(q_nsa, N_HEADS_NSA), pos).reshape(n, t, N_KV_NSA, GROUP_R, HEAD_DIM)
    k_cmp = _heads(k_cmp, N_KV_NSA)
    v_cmp = _heads(v_cmp, N_KV_NSA)
    k_sel = _rope(_heads(k_sel, N_KV_NSA), pos)
    v_sel = _heads(v_sel, N_KV_NSA)
    k_win = _rope(_heads(k_win, N_KV_NSA), pos)
    v_win = _heads(v_win, N_KV_NSA)
    g_nsa = g_nsa.reshape(n, t, N_HEADS_NSA, 3)
    q_sb = _heads(q_sb, N_HEADS_SB)
    k_sb = _heads(k_sb, N_HEADS_SB)
    v_sb = _heads(v_sb, N_HEADS_SB)
    return (q_nsa, k_cmp, v_cmp, k_sel, v_sel, k_win, v_win, g_nsa, q_sb, k_sb, v_sb,
            m_nsa, m_sb, gate1, shift2, scale2, gate2)


def _layer_tail(x, o_nsa, o_sb, m_nsa, m_sb, gate1, shift2, scale2, gate2,
                g_post_mix, g_pre_ffn, g_post_ffn, w_br_nsa, w_br_sb, w_out,
                w_ffn_gate, w_ffn_up, w_ffn_down):
    y = jax.nn.sigmoid(m_nsa) * (o_nsa @ w_br_nsa) + jax.nn.sigmoid(m_sb) * (o_sb @ w_br_sb)
    x = x + gate1 * _rmsnorm(y @ w_out, g_post_mix)
    h = _rmsnorm(x, g_pre_ffn) * (1 + scale2) + shift2
    f = (jax.nn.silu(h @ w_ffn_gate) * (h @ w_ffn_up)) @ w_ffn_down
    return x + gate2 * _rmsnorm(f, g_post_ffn)


def setup_inputs(seed: int = 0) -> dict:
    key = jax.random.key(seed)
    ks = jax.random.split(key, 40)
    f32 = jnp.float32

    def nrm(k, shape, s):
        return jax.random.normal(k, shape, f32) * s

    n_pages = PAST_LEN // PAGE_SIZE
    n_used = DEC_BATCH * n_pages
    n_phys = n_used + (n_used + 3) // 4
    w_buf = min(WINDOW, PAST_LEN)
    page_table = jax.random.permutation(ks[0], n_phys)[:n_used].reshape(DEC_BATCH, n_pages).astype(jnp.int32)
    dm = D_MODEL ** -0.5
    return {
        'x_prompt': nrm(ks[1], (BATCH, SEQ, D_MODEL), 1.0),
        'x_sample': nrm(ks[2], (DEC_BATCH, DEC_SEQ, D_MODEL), 1.0),
        'c_prompt': nrm(ks[3], (BATCH, D_MODEL), 1.0),
        'c_sample': nrm(ks[4], (DEC_BATCH, D_MODEL), 1.0),
        'cache_cmp': nrm(ks[5], (n_phys, PAGE_SIZE, 2, N_KV_NSA, HEAD_DIM), 1.0),
        'cache_sel': nrm(ks[6], (n_phys, PAGE_SIZE, 2, N_KV_NSA, HEAD_DIM), 1.0),
        'cache_sb': nrm(ks[7], (n_phys, PAGE_SIZE, 2, N_HEADS_SB, HEAD_DIM), 1.0),
        'cache_win': nrm(ks[8], (DEC_BATCH, w_buf, 2, N_KV_NSA, HEAD_DIM), 1.0),
        'page_table': page_table,
        'w_ada': nrm(ks[9], (D_MODEL, 6 * D_MODEL), 0.5 * dm),
        'b_ada': nrm(ks[10], (6 * D_MODEL,), 0.01),
        'g_pre_mix': 1.0 + nrm(ks[11], (D_MODEL,), 0.05),
        'g_post_mix': 1.0 + nrm(ks[12], (D_MODEL,), 0.05),
        'g_pre_ffn': 1.0 + nrm(ks[13], (D_MODEL,), 0.05),
        'g_post_ffn': 1.0 + nrm(ks[14], (D_MODEL,), 0.05),
        'w_in': nrm(ks[15], (D_MODEL, D_IN), dm),
        'pe_ck': nrm(ks[16], (L_CMP, HEAD_DIM), 0.5),
        'w_ck1': nrm(ks[17], (L_CMP * HEAD_DIM, CMP_HIDDEN), (L_CMP * HEAD_DIM) ** -0.5),
        'b_ck1': nrm(ks[18], (CMP_HIDDEN,), 0.01),
        'w_ck2': nrm(ks[19], (CMP_HIDDEN, HEAD_DIM), 1.5 * CMP_HIDDEN ** -0.5),
        'pe_cv': nrm(ks[20], (L_CMP, HEAD_DIM), 0.5),
        'w_cv1': nrm(ks[21], (L_CMP * HEAD_DIM, CMP_HIDDEN), (L_CMP * HEAD_DIM) ** -0.5),
        'b_cv1': nrm(ks[22], (CMP_HIDDEN,), 0.01),
        'w_cv2': nrm(ks[23], (CMP_HIDDEN, HEAD_DIM), 1.5 * CMP_HIDDEN ** -0.5),
        'w_br_nsa': nrm(ks[24], (Q_NSA, D_MODEL), Q_NSA ** -0.5),
        'w_br_sb': nrm(ks[25], (QKV_SB, D_MODEL), QKV_SB ** -0.5),
        'w_out': nrm(ks[26], (D_MODEL, D_MODEL), dm),
        'w_ffn_gate': nrm(ks[27], (D_MODEL, D_FF), dm),
        'w_ffn_up': nrm(ks[28], (D_MODEL, D_FF), dm),
        'w_ffn_down': nrm(ks[29], (D_FF, D_MODEL), D_FF ** -0.5),
    }


def reference(x_prompt, x_sample, c_prompt, c_sample, cache_cmp, cache_sel, cache_sb, cache_win,
              page_table, w_ada, b_ada, g_pre_mix, g_post_mix, g_pre_ffn, g_post_ffn, w_in,
              pe_ck, w_ck1, b_ck1, w_ck2, pe_cv, w_cv1, b_cv1, w_cv2,
              w_br_nsa, w_br_sb, w_out, w_ffn_gate, w_ffn_up, w_ffn_down):
    cmp_w = (pe_ck, w_ck1, b_ck1, w_ck2, pe_cv, w_cv1, b_cv1, w_cv2)
    tail_w = (g_post_mix, g_pre_ffn, g_post_ffn, w_br_nsa, w_br_sb, w_out,
              w_ffn_gate, w_ffn_up, w_ffn_down)
    w_buf = cache_win.shape[1]

    n_p, t_p = x_prompt.shape[0], x_prompt.shape[1]
    pos_p = jnp.arange(t_p)
    (qn, kcm, vcm, ksl, vsl, kwn, vwn, gn, qs, ksb, vsb, mn, ms,
     gate1, shift2, scale2, gate2) = _project(x_prompt, c_prompt, pos_p, w_ada, b_ada, g_pre_mix, w_in)
    kc, vc, c_end = _compressed_kv(kcm, vcm, *cmp_w)
    kb, vb = _to_blocks(ksl), _to_blocks(vsl)
    pad_w = ((0, 0), (WINDOW, 0), (0, 0), (0, 0))
    kw_pad, vw_pad = jnp.pad(kwn, pad_w), jnp.pad(vwn, pad_w)

    def block(b):
        b0 = b * QBLK
        qpos = b0 + jnp.arange(QBLK)

        def take(a):
            return lax.dynamic_slice_in_dim(a, b0, QBLK, axis=1)

        kw = lax.dynamic_slice_in_dim(kw_pad, b0, WINDOW + QBLK, axis=1)
        vw = lax.dynamic_slice_in_dim(vw_pad, b0, WINDOW + QBLK, axis=1)
        kwpos = b0 - WINDOW + jnp.arange(WINDOW + QBLK)
        return _token_mixers(take(qn), take(gn), qpos, kc, vc, c_end, kb, vb, kw, vw, kwpos,
                             take(qs), [ksb], [vsb], pos_p)

    o_nsa_b, o_sb_b = lax.map(block, jnp.arange(t_p // QBLK))
    o_nsa = o_nsa_b.transpose(1, 0, 2, 3).reshape(n_p, t_p, Q_NSA)
    o_sb = o_sb_b.transpose(1, 0, 2, 3).reshape(n_p, t_p, QKV_SB)
    y_prompt = _layer_tail(x_prompt, o_nsa, o_sb, mn, ms, gate1, shift2, scale2, gate2, *tail_w)
    new_cmp_prompt = jnp.stack([kcm, vcm], axis=2)
    new_sel_prompt = jnp.stack([ksl, vsl], axis=2)
    new_sb_prompt = jnp.stack([ksb, vsb], axis=2)
    win_rows = jnp.stack([kwn, vwn], axis=2)
    new_win_prompt = jnp.concatenate(
        [jnp.zeros((n_p, w_buf) + win_rows.shape[2:], win_rows.dtype), win_rows], axis=1)[:, -w_buf:]

    n_s, t_s = x_sample.shape[0], x_sample.shape[1]
    pos_s = PAST_LEN + jnp.arange(t_s)
    (qn2, kcm2, vcm2, ksl2, vsl2, kwn2, vwn2, gn2, qs2, ksb2, vsb2, mn2, ms2,
     gate1s, shift2s, scale2s, gate2s) = _project(x_sample, c_sample, pos_s, w_ada, b_ada, g_pre_mix, w_in)

    def gather(cache, kv):
        rows = cache[page_table, :, kv]
        return rows.reshape(page_table.shape[0], -1, cache.shape[3], cache.shape[4])

    k_cmp_all = jnp.concatenate([gather(cache_cmp, 0), kcm2], axis=1)
    v_cmp_all = jnp.concatenate([gather(cache_cmp, 1), vcm2], axis=1)
    kc2, vc2, c_end2 = _compressed_kv(k_cmp_all, v_cmp_all, *cmp_w)
    kb2 = _to_blocks(jnp.concatenate([gather(cache_sel, 0), ksl2], axis=1))
    vb2 = _to_blocks(jnp.concatenate([gather(cache_sel, 1), vsl2], axis=1))
    kw2 = jnp.concatenate([cache_win[:, :, 0], kwn2], axis=1)
    vw2 = jnp.concatenate([cache_win[:, :, 1], vwn2], axis=1)
    kwpos2 = jnp.concatenate([PAST_LEN - w_buf + jnp.arange(w_buf), pos_s])
    kpos_sb2 = jnp.arange(PAST_LEN + t_s)
    o_nsa2, o_sb2 = _token_mixers(qn2, gn2, pos_s, kc2, vc2, c_end2, kb2, vb2, kw2, vw2, kwpos2,
                                  qs2, [gather(cache_sb, 0), ksb2], [gather(cache_sb, 1), vsb2], kpos_sb2)
    y_sample = _layer_tail(x_sample, o_nsa2, o_sb2, mn2, ms2, gate1s, shift2s, scale2s, gate2s, *tail_w)
    new_cmp_sample = jnp.stack([kcm2, vcm2], axis=2)
    new_sel_sample = jnp.stack([ksl2, vsl2], axis=2)
    new_sb_sample = jnp.stack([ksb2, vsb2], axis=2)
    new_win_sample = jnp.concatenate([cache_win, jnp.stack([kwn2, vwn2], axis=2)], axis=1)[:, -w_buf:]

    return (y_prompt, y_sample, new_cmp_prompt, new_sel_prompt, new_sb_prompt, new_win_prompt,
            new_cmp_sample, new_sel_sample, new_sb_sample, new_win_sample)
```

```python
import functools

import numpy as np
import jax
import jax.numpy as jnp
from jax import lax
from jax.experimental import pallas as pl
from jax.experimental.pallas import tpu as pltpu

BF = jnp.bfloat16
F32 = jnp.float32

HEAD_DIM = 128
N_KV = 2
GROUP_R = 4
N_H_NSA = N_KV * GROUP_R
N_H_SB = 8
L_CMP = 32
STRIDE = 16
SEL_BLOCK = 64
N_SELECT = 16
WINDOW = 512
PAGE = 128
CMP_HIDDEN = 256
ROPE_THETA = 10000.0
EPS = 1e-6
SCALE = HEAD_DIM ** -0.5
NEG = -1e30
VMEM_LIMIT = 56 * 1024 * 1024


def _cparams(n_axes):
    return pltpu.CompilerParams(dimension_semantics=("arbitrary",) * n_axes,
                                vmem_limit_bytes=VMEM_LIMIT)


def _dot(a, b):
    return jnp.dot(a, b, preferred_element_type=F32)


def _dot_nt(a, b):
    return lax.dot_general(a, b, (((1,), (1,)), ((), ())), preferred_element_type=F32)


def _split3(x):
    hi = x.astype(BF)
    r = x - hi.astype(F32)
    mid = r.astype(BF)
    lo = (r - mid.astype(F32)).astype(BF)
    return hi, mid, lo


def _softmax_rows(s, valid):
    s = jnp.where(valid, s, NEG)
    m = jnp.max(s, axis=-1, keepdims=True)
    m = jnp.where(m > 0.5 * NEG, m, 0.0)
    p = jnp.where(valid, jnp.exp(s - m), 0.0)
    return p / jnp.maximum(jnp.sum(p, axis=-1, keepdims=True), 1e-30)


def _topk_select(val, k, lane):
    big = jnp.int32(1 << 30)
    sel = jnp.zeros(val.shape, F32)
    picks = []
    for _ in range(k):
        m = jnp.max(val, axis=-1, keepdims=True)
        idx = jnp.min(jnp.where(val == m, lane, big), axis=-1, keepdims=True)
        hit = lane == idx
        sel = jnp.where(hit, 1.0, sel)
        val = jnp.where(hit, -2.0, val)
        picks.append(idx)
    return sel, picks


def _block_scores(imp, mmat_ref):
    hi, mid, lo = _split3(imp)
    m = mmat_ref[...]
    return _dot(hi, m) + _dot(mid, m) + _dot(lo, m)


def _rope_tables(pos):
    half = HEAD_DIM // 2
    inv = ROPE_THETA ** (-2.0 * jnp.arange(half, dtype=F32) / HEAD_DIM)
    ang = pos.astype(F32)[:, None] * inv[None, :]
    c, s = jnp.cos(ang), jnp.sin(ang)
    return jnp.concatenate([c, c], axis=1), jnp.concatenate([-s, s], axis=1)


def _rope(x, cos, sin_signed):
    return x * cos + pltpu.roll(x, HEAD_DIM // 2, 1) * sin_signed


def _score_matrix(n_c_pad, n_blk_pad):
    j = np.arange(n_c_pad)[:, None]
    b = np.arange(n_blk_pad)[None, :]
    spb = SEL_BLOCK // STRIDE
    m = ((j == spb * b - 1).astype(np.float32) + (j == spb * b + spb - 1)
         + 2.0 * ((j >= spb * b) & (j <= spb * b + spb - 2)))
    return jnp.asarray(m, BF)


def _ada_kernel(c_ref, w_ref, b_ref, o_ref):
    o_ref[...] = _dot(c_ref[...], w_ref[...].astype(BF)) + b_ref[...]


def _adaln(c_pad, w_ada, b_ada):
    m, d = c_pad.shape
    n = w_ada.shape[1]
    tn = 1536
    return pl.pallas_call(
        _ada_kernel,
        out_shape=jax.ShapeDtypeStruct((m, n), F32),
        grid=(n // tn,),
        in_specs=[pl.BlockSpec((m, d), lambda j: (0, 0)),
                  pl.BlockSpec((d, tn), lambda j: (0, j)),
                  pl.BlockSpec((1, tn), lambda j: (0, j))],
        out_specs=pl.BlockSpec((m, tn), lambda j: (0, j)),
        compiler_params=_cparams(1),
        name="adaln",
    )(c_pad.astype(BF), w_ada, b_ada.reshape(1, n))


def _rows(ref):
    v = ref[...]
    return v.reshape(v.shape[-2], v.shape[-1])


def _prenorm_kernel(x_ref, g_ref, sc_ref, sh_ref, o_ref):
    x = x_ref[...]
    y = x * lax.rsqrt(jnp.mean(x * x, axis=-1, keepdims=True) + EPS)
    o_ref[...] = (y * g_ref[...] * (1.0 + _rows(sc_ref)) + _rows(sh_ref)).astype(o_ref.dtype)


def _mod_spec(tm, d, k, row_of_tile):
    if row_of_tile is None:
        return pl.BlockSpec((tm, d), lambda i: (i, k))
    return pl.BlockSpec((1, 1, d), lambda i: (row_of_tile(i), 0, k))


def _prenorm(x2d, g, mod, k_scale, k_shift, tm, row_of_tile):
    m, d = x2d.shape
    return pl.pallas_call(
        _prenorm_kernel,
        out_shape=jax.ShapeDtypeStruct((m, d), BF),
        grid=(m // tm,),
        in_specs=[pl.BlockSpec((tm, d), lambda i: (i, 0)),
                  pl.BlockSpec((1, d), lambda i: (0, 0)),
                  _mod_spec(tm, d, k_scale, row_of_tile),
                  _mod_spec(tm, d, k_shift, row_of_tile)],
        out_specs=pl.BlockSpec((tm, d), lambda i: (i, 0)),
        compiler_params=_cparams(1),
        name="prenorm",
    )(x2d, g.reshape(1, d), mod, mod)


def _proj_kernel(h_ref, w_ref, cos_ref, sin_ref, *o_refs, rope_heads, tn):
    y = _dot(h_ref[...], w_ref[...])
    if rope_heads:
        cos, sin = cos_ref[...], sin_ref[...]
        parts = []
        for hd in range(tn // HEAD_DIM):
            yh = y[:, hd * HEAD_DIM:(hd + 1) * HEAD_DIM]
            parts.append(_rope(yh, cos, sin) if hd < rope_heads else yh)
        y = jnp.concatenate(parts, axis=1)
    for o_ref in o_refs:
        o_ref[...] = y.astype(o_ref.dtype)


def _project(h, w, cos, sin, tm, rope_heads, out_dtypes):
    m, d = h.shape
    n = w.shape[1]
    tn = min(n, 512)
    t_tiles = cos.shape[0] // tm
    outs = pl.pallas_call(
        functools.partial(_proj_kernel, rope_heads=rope_heads, tn=tn),
        out_shape=[jax.ShapeDtypeStruct((m, n), dt) for dt in out_dtypes],
        grid=(n // tn, m // tm),
        in_specs=[pl.BlockSpec((tm, d), lambda j, i: (i, 0)),
                  pl.BlockSpec((d, tn), lambda j, i: (0, j)),
                  pl.BlockSpec((tm, HEAD_DIM), lambda j, i: (i % t_tiles, 0)),
                  pl.BlockSpec((tm, HEAD_DIM), lambda j, i: (i % t_tiles, 0))],
        out_specs=[pl.BlockSpec((tm, tn), lambda j, i: (i, j)) for _ in out_dtypes],
        compiler_params=_cparams(2),
        name="project",
    )(h, w, cos, sin)
    return outs


def _gelu(x):
    return 0.5 * x * (1.0 + jnp.tanh(0.7978845608028654 * (x + 0.044715 * x * x * x)))


def _cmp_bias(pe_ref, w1_ref, b1_ref):
    pb = _dot(pe_ref[0], w1_ref[0])
    return pb[0:1, :CMP_HIDDEN] + pb[1:2, CMP_HIDDEN:] + b1_ref[0]


def _cmp_finish(first, second, bias, w2, cos, sin, is_key, nseg):
    nxt = pltpu.roll(second, nseg - 1, 0)
    hid = _gelu(first + nxt + bias).astype(BF)
    out = _dot(hid, w2)
    out = jnp.where(is_key, _rope(out, cos, sin), out)
    row = lax.broadcasted_iota(jnp.int32, out.shape, 0)
    return jnp.where(row < nseg - 1, out, 0.0)


def _cmp_prompt_kernel(x_ref, w1_ref, pe_ref, b1_ref, w2_ref, cos_ref, sin_ref, o_ref, *, nseg):
    c = pl.program_id(1)
    segs = jnp.concatenate(
        [x_ref[0, pl.ds(r, nseg, stride=STRIDE), :] for r in range(STRIDE)], axis=1).astype(BF)
    fs = _dot(segs, w1_ref[0])
    bias = _cmp_bias(pe_ref, w1_ref, b1_ref)
    out = _cmp_finish(fs[:, :CMP_HIDDEN], fs[:, CMP_HIDDEN:], bias, w2_ref[0],
                      cos_ref[...], sin_ref[...], c < N_KV, nseg)
    o_ref[0, 0] = out.astype(o_ref.dtype)


def _cmp_weights(pe_ck, w_ck1, b_ck1, w_ck2, pe_cv, w_cv1, b_cv1, w_cv2):
    half = STRIDE * HEAD_DIM

    def w1cat(w):
        return jnp.concatenate([w[:half], w[half:]], axis=1)

    w1 = jnp.stack([w1cat(w_ck1), w1cat(w_cv1)]).astype(BF)
    pe = jnp.stack([pe_ck.reshape(2, half), pe_cv.reshape(2, half)])
    pe = jnp.pad(pe, ((0, 0), (0, 6), (0, 0))).astype(BF)
    b1 = jnp.stack([b_ck1, b_cv1]).reshape(2, 1, CMP_HIDDEN)
    w2 = jnp.stack([w_ck2, w_cv2]).astype(BF)
    return w1, pe, b1, w2


def _compress_prompt(kv_cmp, cmp_w, n, t):
    w1, pe, b1, w2 = cmp_w
    nseg = t // STRIDE
    cos, sin = _rope_tables(jnp.arange(nseg) * STRIDE)
    x3 = kv_cmp.reshape(n, t, 4 * HEAD_DIM)
    return pl.pallas_call(
        functools.partial(_cmp_prompt_kernel, nseg=nseg),
        out_shape=jax.ShapeDtypeStruct((n, 4, nseg, HEAD_DIM), BF),
        grid=(n, 4),
        in_specs=[pl.BlockSpec((1, t, HEAD_DIM), lambda b, c: (b, 0, c)),
                  pl.BlockSpec((1,) + w1.shape[1:], lambda b, c: (c // N_KV, 0, 0)),
                  pl.BlockSpec((1,) + pe.shape[1:], lambda b, c: (c // N_KV, 0, 0)),
                  pl.BlockSpec((1,) + b1.shape[1:], lambda b, c: (c // N_KV, 0, 0)),
                  pl.BlockSpec((1,) + w2.shape[1:], lambda b, c: (c // N_KV, 0, 0)),
                  pl.BlockSpec((nseg, HEAD_DIM), lambda b, c: (0, 0)),
                  pl.BlockSpec((nseg, HEAD_DIM), lambda b, c: (0, 0))],
        out_specs=pl.BlockSpec((1, 1, nseg, HEAD_DIM), lambda b, c: (b, c, 0, 0)),
        compiler_params=_cparams(2),
        name="compress_prompt",
    )(x3, w1, pe, b1, w2, cos, sin)


def _flash_step(qs, k, v, valid, m, l, acc):
    tq, tk = valid.shape
    s = (_dot_nt(qs, k) * SCALE).reshape(GROUP_R, tq, tk)
    s = jnp.where(valid[None], s, NEG).reshape(GROUP_R * tq, tk)
    m_new = jnp.maximum(m, jnp.max(s, axis=-1, keepdims=True))
    alpha = jnp.exp(m - m_new)
    p = jnp.exp(s - m_new).reshape(GROUP_R, tq, tk)
    p = jnp.where(valid[None], p, 0.0).reshape(GROUP_R * tq, tk)
    l = alpha * l + jnp.sum(p, axis=-1, keepdims=True)
    acc = alpha * acc + _dot(p.astype(BF), v)
    return m_new, l, acc


def _nsa_prompt_kernel(q_ref, kc_ref, vc_ref, ks_ref, vs_ref, kw_ref, vw_ref, g_ref, mm_ref,
                       o_ref, *, tq, n_c, n_blk):
    g_idx = pl.program_id(1)
    qi = pl.program_id(2)
    q = q_ref[...]
    qs = jnp.concatenate([q[:, h * HEAD_DIM:(h + 1) * HEAD_DIM] for h in range(GROUP_R)], axis=0)
    rows = GROUP_R * tq
    qpos = qi * tq + lax.broadcasted_iota(jnp.int32, (tq, 1), 0)

    n_cp = kc_ref.shape[2]
    s = (_dot_nt(qs, kc_ref[0, 0]) * SCALE).reshape(GROUP_R, tq, n_cp)
    jc = lax.broadcasted_iota(jnp.int32, (tq, n_cp), 1)
    valid_c = (jc * STRIDE + (L_CMP - 1) <= qpos) & (jc < n_c)
    p_c = _softmax_rows(s, valid_c[None])
    o_c = _dot(p_c.reshape(rows, n_cp).astype(BF), vc_ref[0, 0])
    imp = jnp.sum(p_c, axis=0)

    score = _block_scores(imp, mm_ref)
    n_lane = score.shape[1]
    blk = lax.broadcasted_iota(jnp.int32, (tq, n_lane), 1)
    cur = qpos // SEL_BLOCK
    forced = (blk == 0) | (blk == cur) | (blk == cur - 1)
    val = jnp.where((blk > cur) | (blk >= n_blk), -1.0, jnp.where(forced, 3e38, score))
    sel, _ = _topk_select(val, min(N_SELECT, n_blk), blk)
    sel = sel.astype(BF)

    lane_k = lax.broadcasted_iota(jnp.int32, (tq, tq), 1)
    eb = lax.broadcasted_iota(jnp.int32, (n_lane, tq), 0)
    ej = lax.broadcasted_iota(jnp.int32, (n_lane, tq), 1) // SEL_BLOCK
    init = (jnp.full((rows, 1), NEG, F32), jnp.zeros((rows, 1), F32),
            jnp.zeros((rows, HEAD_DIM), F32))

    def sel_body(kt, carry):
        k0 = pl.multiple_of(kt * tq, tq)
        expand = jnp.where(eb == ej + kt * (tq // SEL_BLOCK), 1.0, 0.0).astype(BF)
        chosen = _dot(sel, expand) > 0.5
        valid = chosen & (k0 + lane_k <= qpos)
        return _flash_step(qs, ks_ref[pl.ds(k0, tq), :], vs_ref[pl.ds(k0, tq), :], valid, *carry)

    _, l_s, acc_s = lax.fori_loop(0, qi + 1, sel_body, init)
    o_s = acc_s / jnp.maximum(l_s, 1e-30)

    def win_body(kt, carry):
        k0 = pl.multiple_of(kt * tq, tq)
        dist = qpos - (k0 + lane_k)
        valid = (dist >= 0) & (dist < WINDOW)
        return _flash_step(qs, kw_ref[pl.ds(k0, tq), :], vw_ref[pl.ds(k0, tq), :], valid, *carry)

    _, l_w, acc_w = lax.fori_loop(jnp.maximum(qi - WINDOW // tq, 0), qi + 1, win_body, init)
    o_w = acc_w / jnp.maximum(l_w, 1e-30)

    gate = jax.nn.sigmoid(g_ref[...])
    lane_g = lax.broadcasted_iota(jnp.int32, gate.shape, 1)
    outs = []
    for h in range(GROUP_R):
        base = (g_idx * GROUP_R + h) * 3
        g3 = [jnp.sum(jnp.where(lane_g == base + b, gate, 0.0), axis=-1, keepdims=True)
              for b in range(3)]
        sl = slice(h * tq, (h + 1) * tq)
        outs.append(g3[0] * o_c[sl] + g3[1] * o_s[sl] + g3[2] * o_w[sl])
    o_ref[...] = jnp.concatenate(outs, axis=1).astype(o_ref.dtype)


def _nsa_prompt(q_nsa, kvc, sel_bf, win_bf, g_nsa, n, t, tq):
    nseg = kvc.shape[2]
    n_c = nseg - 1
    n_blk = -(-t // SEL_BLOCK)
    qt = t // tq
    gw = GROUP_R * HEAD_DIM
    mmat = _score_matrix(nseg, 128)
    kv_spec = lambda col: pl.BlockSpec((t, HEAD_DIM), lambda b, g, i: (b, col + g))
    return pl.pallas_call(
        functools.partial(_nsa_prompt_kernel, tq=tq, n_c=n_c, n_blk=n_blk),
        out_shape=jax.ShapeDtypeStruct((n * t, N_H_NSA * HEAD_DIM), BF),
        grid=(n, N_KV, qt),
        in_specs=[pl.BlockSpec((tq, gw), lambda b, g, i: (b * qt + i, g)),
                  pl.BlockSpec((1, 1, nseg, HEAD_DIM), lambda b, g, i: (b, g, 0, 0)),
                  pl.BlockSpec((1, 1, nseg, HEAD_DIM), lambda b, g, i: (b, N_KV + g, 0, 0)),
                  kv_spec(0), kv_spec(N_KV), kv_spec(0), kv_spec(N_KV),
                  pl.BlockSpec((tq, HEAD_DIM), lambda b, g, i: (b * qt + i, 0)),
                  pl.BlockSpec(mmat.shape, lambda b, g, i: (0, 0))],
        out_specs=pl.BlockSpec((tq, gw), lambda b, g, i: (b * qt + i, g)),
        compiler_params=_cparams(3),
        name="nsa_prompt",
    )(q_nsa, kvc, kvc, sel_bf, sel_bf, win_bf, win_bf, g_nsa, mmat)


def _softplus(z):
    return jnp.maximum(z, 0.0) + jnp.log1p(jnp.exp(-jnp.abs(z)))


def _strict_upper(n):
    return jnp.where(lax.broadcasted_iota(jnp.int32, (n, n), 0)
                     > lax.broadcasted_iota(jnp.int32, (n, n), 1), 1.0, 0.0).astype(BF)


def _suffix_sum(lm, u):
    hi = lm.astype(BF)
    lo = (lm - hi.astype(F32)).astype(BF)
    return _dot(hi, u) + _dot(lo, u)


def _sb_prompt_kernel(q_ref, k_ref, v_ref, o_ref, *, tq):
    qi = pl.program_id(2)
    q = q_ref[...]
    u = _strict_upper(tq)
    qpos = qi * tq + lax.broadcasted_iota(jnp.int32, (tq, 1), 0)
    lane = lax.broadcasted_iota(jnp.int32, (tq, tq), 1)

    def body(step, carry):
        run, acc = carry
        k0 = pl.multiple_of((qi - step) * tq, tq)
        z = _dot_nt(q, k_ref[pl.ds(k0, tq), :]) * SCALE
        causal = k0 + lane < qpos
        sp = _softplus(z)
        lm = jnp.where(causal, -sp, 0.0)
        rest = _suffix_sum(lm, u) + run
        a = jnp.where(causal, jnp.exp(z - sp + rest), 0.0)
        acc = acc + _dot(a.astype(BF), v_ref[pl.ds(k0, tq), :])
        return run + jnp.sum(lm, axis=-1, keepdims=True), acc

    _, acc = lax.fori_loop(0, qi + 1, body,
                           (jnp.zeros((tq, 1), F32), jnp.zeros((tq, HEAD_DIM), F32)))
    o_ref[...] = acc.astype(o_ref.dtype)


def _sb_prompt(q_sb, kv_sb_bf, n, t, tq):
    qt = t // tq
    return pl.pallas_call(
        functools.partial(_sb_prompt_kernel, tq=tq),
        out_shape=jax.ShapeDtypeStruct((n * t, N_H_SB * HEAD_DIM), BF),
        grid=(n, N_H_SB, qt),
        in_specs=[pl.BlockSpec((tq, HEAD_DIM), lambda b, h, i: (b * qt + i, h)),
                  pl.BlockSpec((t, HEAD_DIM), lambda b, h, i: (b, h)),
                  pl.BlockSpec((t, HEAD_DIM), lambda b, h, i: (b, N_H_SB + h))],
        out_specs=pl.BlockSpec((tq, HEAD_DIM), lambda b, h, i: (b * qt + i, h)),
        compiler_params=_cparams(3),
        name="sb_prompt",
    )(q_sb, kv_sb_bf, kv_sb_bf)


def _sb_sample_kernel(pt_ref, q_ref, *refs, n_pg):
    page_refs = refs[:n_pg]
    o_ref = refs[n_pg]
    run_ref, acc_ref = refs[n_pg + 1:]
    j = pl.program_id(1)

    @pl.when(j == 0)
    def _():
        run_ref[...] = jnp.zeros_like(run_ref)
        acc_ref[...] = jnp.zeros_like(acc_ref)

    q = q_ref[0]
    u = _strict_upper(PAGE)
    row = lax.broadcasted_iota(jnp.int32, (N_H_SB, PAGE), 0)
    hw = N_H_SB * HEAD_DIM
    run = run_ref[...]
    acc = acc_ref[...]
    for pg in reversed(range(n_pg)):
        ref = page_refs[pg]
        z = jnp.zeros((N_H_SB, PAGE), F32)
        for h in range(N_H_SB):
            kh = ref[0, :, h * HEAD_DIM:(h + 1) * HEAD_DIM].astype(BF)
            z = jnp.where(row == h, _dot_nt(q, kh), z)
        z = z * SCALE
        sp = _softplus(z)
        lm = -sp
        a = jnp.exp(z - sp + _suffix_sum(lm, u) + run).astype(BF)
        for h in range(N_H_SB):
            vh = ref[0, :, hw + h * HEAD_DIM:hw + (h + 1) * HEAD_DIM].astype(BF)
            acc = acc + jnp.where(row == h, _dot(a, vh), 0.0)
        run = run + jnp.sum(lm, axis=-1, keepdims=True)
    run_ref[...] = run
    acc_ref[...] = acc

    @pl.when(j == pl.num_programs(1) - 1)
    def _():
        o_ref[0] = acc.astype(o_ref.dtype)


def _sb_sample(q_sb, cache_sb, page_table):
    bsz, n_pages = page_table.shape
    n_phys = cache_sb.shape[0]
    n_pg = 4
    n_chunk = n_pages // n_pg
    hw = N_H_SB * HEAD_DIM
    cache = cache_sb.reshape(n_phys, PAGE, 2 * hw)

    def page_spec(pg):
        return pl.BlockSpec(
            (1, PAGE, 2 * hw),
            lambda b, j, pt: (pt[b * n_pages + (n_chunk - 1 - j) * n_pg + pg], 0, 0))

    out = pl.pallas_call(
        functools.partial(_sb_sample_kernel, n_pg=n_pg),
        out_shape=jax.ShapeDtypeStruct((bsz, N_H_SB, HEAD_DIM), BF),
        grid_spec=pltpu.PrefetchScalarGridSpec(
            num_scalar_prefetch=1,
            grid=(bsz, n_chunk),
            in_specs=[pl.BlockSpec((1, N_H_SB, HEAD_DIM), lambda b, j, pt: (b, 0, 0))]
            + [page_spec(pg) for pg in range(n_pg)],
            out_specs=pl.BlockSpec((1, N_H_SB, HEAD_DIM), lambda b, j, pt: (b, 0, 0)),
            scratch_shapes=[pltpu.VMEM((N_H_SB, 1), F32), pltpu.VMEM((N_H_SB, HEAD_DIM), F32)]),
        compiler_params=_cparams(2),
        name="sb_sample",
    )(page_table.reshape(-1), q_sb.reshape(bsz, N_H_SB, HEAD_DIM), *([cache] * n_pg))
    return out.reshape(bsz, hw)


def _cmp_sample_kernel(pt_ref, q_ref, w1_ref, pe_ref, b1_ref, w2_ref, cos_ref, sin_ref, mm_ref,
                       *refs, n_pg, nseg, n_blk, q_pos):
    n_col = 2 * N_KV
    page_refs = refs[:n_pg * n_col]
    oc_ref, idx_ref = refs[n_pg * n_col:n_pg * n_col + 2]
    fs_ref = refs[n_pg * n_col + 2]
    j = pl.program_id(1)
    spp = PAGE // STRIDE
    for c in range(n_col):
        segs = []
        for pg in range(n_pg):
            segs.append(jnp.concatenate(
                [page_refs[pg * n_col + c][0, pl.ds(r, spp, stride=STRIDE), :]
                 for r in range(STRIDE)], axis=1))
        segs = jnp.concatenate(segs, axis=0).astype(BF)
        row0 = pl.multiple_of(j * (n_pg * spp), n_pg * spp)
        fs_ref[c, pl.ds(row0, n_pg * spp), :] = _dot(segs, w1_ref[c // N_KV])

    @pl.when(j == pl.num_programs(1) - 1)
    def _():
        n_c = nseg - 1
        cos, sin = cos_ref[...], sin_ref[...]
        comp = []
        for c in range(2 * N_KV):
            kv = c // N_KV
            pb = _dot(pe_ref[kv], w1_ref[kv])
            bias = pb[0:1, :CMP_HIDDEN] + pb[1:2, CMP_HIDDEN:] + b1_ref[kv]
            fs = fs_ref[c]
            comp.append(_cmp_finish(fs[:, :CMP_HIDDEN], fs[:, CMP_HIDDEN:], bias, w2_ref[kv],
                                    cos, sin, kv == 0, nseg).astype(BF))
        jc = lax.broadcasted_iota(jnp.int32, (8, nseg), 1)
        valid = (jc * STRIDE + (L_CMP - 1) <= q_pos) & (jc < n_c)
        head = lax.broadcasted_iota(jnp.int32, (8, nseg), 0)
        n_lane = mm_ref.shape[1]
        blk = lax.broadcasted_iota(jnp.int32, (8, n_lane), 1)
        cur = q_pos // SEL_BLOCK
        forced = (blk == 0) | (blk == cur) | (blk == cur - 1)
        lane_o = lax.broadcasted_iota(jnp.int32, (8, HEAD_DIM), 1)
        for g in range(N_KV):
            qg = q_ref[0, g]
            s = _dot_nt(qg, comp[g]) * SCALE
            p = _softmax_rows(s, valid)
            oc_ref[0, g] = _dot(p.astype(BF), comp[N_KV + g])
            imp = jnp.sum(jnp.where(head < GROUP_R, p, 0.0), axis=0, keepdims=True)
            score = _block_scores(jnp.broadcast_to(imp, (8, nseg)), mm_ref)
            val = jnp.where((blk > cur) | (blk >= n_blk), -1.0, jnp.where(forced, 3e38, score))
            _, picks = _topk_select(val, min(N_SELECT, n_blk), blk)
            row = jnp.zeros((8, HEAD_DIM), jnp.int32)
            for kk, pick in enumerate(picks):
                row = jnp.where(lane_o == kk, pick, row)
            idx_ref[0, g] = row


def _cmp_sample(q_nsa, cache_cmp, page_table, cmp_w, q_pos):
    w1, pe, b1, w2 = cmp_w
    bsz, n_pages = page_table.shape
    n_phys = cache_cmp.shape[0]
    n_pg = min(16, n_pages)
    n_chunk = n_pages // n_pg
    nseg = n_pages * PAGE // STRIDE
    n_blk = -(-(n_pages * PAGE + 1) // SEL_BLOCK)
    n_lane = -(-n_blk // 128) * 128
    mmat = _score_matrix(nseg, n_lane)
    cos, sin = _rope_tables(jnp.arange(nseg) * STRIDE)
    cache = cache_cmp.reshape(n_phys, PAGE, 4 * HEAD_DIM)
    q4 = q_nsa.reshape(bsz, N_KV, GROUP_R, HEAD_DIM)
    q4 = jnp.pad(q4, ((0, 0), (0, 0), (0, 8 - GROUP_R), (0, 0)))

    def page_spec(pg, c):
        return pl.BlockSpec((1, PAGE, HEAD_DIM),
                            lambda b, j, pt: (pt[b * n_pages + j * n_pg + pg], 0, c))

    page_specs = [page_spec(pg, c) for pg in range(n_pg) for c in range(2 * N_KV)]
    full = lambda a: pl.BlockSpec(a.shape, lambda b, j, pt: (0,) * a.ndim)
    return pl.pallas_call(
        functools.partial(_cmp_sample_kernel, n_pg=n_pg, nseg=nseg, n_blk=n_blk, q_pos=q_pos),
        out_shape=[jax.ShapeDtypeStruct((bsz, N_KV, 8, HEAD_DIM), F32),
                   jax.ShapeDtypeStruct((bsz, N_KV, 8, HEAD_DIM), jnp.int32)],
        grid_spec=pltpu.PrefetchScalarGridSpec(
            num_scalar_prefetch=1,
            grid=(bsz, n_chunk),
            in_specs=[pl.BlockSpec((1, N_KV, 8, HEAD_DIM), lambda b, j, pt: (b, 0, 0, 0)),
                      full(w1), full(pe), full(b1), full(w2), full(cos), full(sin), full(mmat)]
            + page_specs,
            out_specs=[pl.BlockSpec((1, N_KV, 8, HEAD_DIM), lambda b, j, pt: (b, 0, 0, 0)),
                       pl.BlockSpec((1, N_KV, 8, HEAD_DIM), lambda b, j, pt: (b, 0, 0, 0))],
            scratch_shapes=[pltpu.VMEM((2 * N_KV, nseg, 2 * CMP_HIDDEN), F32)]),
        compiler_params=_cparams(2),
        name="cmp_sample",
    )(page_table.reshape(-1), q4, w1, pe, b1, w2, cos, sin, mmat,
      *([cache] * len(page_specs)))


def _sel_sample_kernel(idx_ref, pt_ref, q_ref, new_ref, *refs, n_sel, last_blk):
    k_refs = refs[:n_sel]
    v_refs = refs[n_sel:2 * n_sel]
    o_ref = refs[2 * n_sel]
    b = pl.program_id(0)
    g = pl.program_id(1)
    q = q_ref[0, 0]
    keys = jnp.concatenate([r[0] for r in k_refs], axis=0).astype(BF)
    vals = jnp.concatenate([r[0] for r in v_refs], axis=0).astype(BF)
    n_key = n_sel * SEL_BLOCK
    s = _dot_nt(q, keys) * SCALE
    slot = lax.broadcasted_iota(jnp.int32, (8, n_key), 1) // SEL_BLOCK
    valid = jnp.zeros((8, n_key), jnp.bool_)
    has_new = jnp.zeros((8, 1), jnp.bool_)
    for kk in range(n_sel):
        bi = idx_ref[(b * N_KV + g) * n_sel + kk]
        valid = valid | ((slot == kk) & (bi < last_blk))
        has_new = has_new | (bi == last_blk)
    k_new = new_ref[0, 0][:, :HEAD_DIM]
    v_new = new_ref[0, 1][:, :HEAD_DIM]
    s_new = jnp.sum(q.astype(F32) * k_new.astype(BF).astype(F32), axis=-1, keepdims=True) * SCALE
    s = jnp.where(valid, s, NEG)
    s_new = jnp.where(has_new, s_new, NEG)
    m = jnp.maximum(jnp.max(s, axis=-1, keepdims=True), s_new)
    m = jnp.where(m > 0.5 * NEG, m, 0.0)
    p = jnp.where(valid, jnp.exp(s - m), 0.0)
    p_new = jnp.where(has_new, jnp.exp(s_new - m), 0.0)
    den = jnp.maximum(jnp.sum(p, axis=-1, keepdims=True) + p_new, 1e-30)
    p = p / den
    p_new = p_new / den
    o_ref[0, 0] = (_dot(p.astype(BF), vals)
                   + p_new.astype(BF).astype(F32) * v_new.astype(BF).astype(F32))


def _sel_sample(q_nsa, idx, new_sel, cache_sel, page_table):
    bsz, n_pages = page_table.shape
    n_phys = cache_sel.shape[0]
    n_sel = idx.shape[2]
    halves = PAGE // SEL_BLOCK
    last_blk = n_pages * halves
    cache = cache_sel.reshape(n_phys * halves, SEL_BLOCK, 4 * HEAD_DIM)
    q4 = q_nsa.reshape(bsz, N_KV, GROUP_R, HEAD_DIM)
    q4 = jnp.pad(q4, ((0, 0), (0, 0), (0, 8 - GROUP_R), (0, 0)))
    new4 = new_sel.reshape(bsz, 2, 1, N_KV * HEAD_DIM)

    def blk_spec(kk, col):
        def imap(b, g, idx_r, pt_r):
            bi = jnp.minimum(idx_r[(b * N_KV + g) * n_sel + kk], last_blk - 1)
            return (pt_r[b * n_pages + bi // halves] * halves + bi % halves, 0, col + g)
        return pl.BlockSpec((1, SEL_BLOCK, HEAD_DIM), imap)

    return pl.pallas_call(
        functools.partial(_sel_sample_kernel, n_sel=n_sel, last_blk=last_blk),
        out_shape=jax.ShapeDtypeStruct((bsz, N_KV, 8, HEAD_DIM), F32),
        grid_spec=pltpu.PrefetchScalarGridSpec(
            num_scalar_prefetch=2,
            grid=(bsz, N_KV),
            in_specs=[pl.BlockSpec((1, 1, 8, HEAD_DIM), lambda b, g, i_r, p_r: (b, g, 0, 0)),
                      pl.BlockSpec((1, 2, 1, HEAD_DIM), lambda b, g, i_r, p_r: (b, 0, 0, g))]
            + [blk_spec(kk, 0) for kk in range(n_sel)]
            + [blk_spec(kk, N_KV) for kk in range(n_sel)],
            out_specs=pl.BlockSpec((1, 1, 8, HEAD_DIM), lambda b, g, i_r, p_r: (b, g, 0, 0))),
        compiler_params=_cparams(2),
        name="sel_sample",
    )(idx.reshape(-1), page_table.reshape(-1), q4, new4, *([cache] * (2 * n_sel)))


def _win_sample_kernel(q_ref, win_ref, new_ref, oc_ref, os_ref, g_ref, o_ref, nw_ref, *, w_buf):
    blk = win_ref[0]
    new = new_ref[0]
    i = lax.broadcasted_iota(jnp.int32, (8, w_buf), 1)
    valid = (w_buf - i >= 0) & (w_buf - i < WINDOW)
    gate = jax.nn.sigmoid(g_ref[0])
    lane_g = lax.broadcasted_iota(jnp.int32, gate.shape, 1)
    row8 = lax.broadcasted_iota(jnp.int32, (8, HEAD_DIM), 0)
    outs = []
    for g in range(N_KV):
        q = q_ref[0, g]
        k = blk[:, g * HEAD_DIM:(g + 1) * HEAD_DIM].astype(BF)
        v = blk[:, (N_KV + g) * HEAD_DIM:(N_KV + g + 1) * HEAD_DIM].astype(BF)
        k_new = new[:, g * HEAD_DIM:(g + 1) * HEAD_DIM].astype(BF).astype(F32)
        v_new = new[:, (N_KV + g) * HEAD_DIM:(N_KV + g + 1) * HEAD_DIM].astype(BF).astype(F32)
        s = jnp.where(valid, _dot_nt(q, k) * SCALE, NEG)
        s_new = jnp.sum(q.astype(F32) * k_new, axis=-1, keepdims=True) * SCALE
        m = jnp.maximum(jnp.max(s, axis=-1, keepdims=True), s_new)
        p = jnp.where(valid, jnp.exp(s - m), 0.0)
        p_new = jnp.exp(s_new - m)
        den = jnp.sum(p, axis=-1, keepdims=True) + p_new
        p = p / den
        p_new = p_new / den
        o_w = _dot(p.astype(BF), v) + p_new.astype(BF).astype(F32) * v_new
        o_c, o_s = oc_ref[0, g], os_ref[0, g]
        comb = jnp.zeros((8, HEAD_DIM), F32)
        for h in range(GROUP_R):
            base = (g * GROUP_R + h) * 3
            g3 = [jnp.sum(jnp.where(lane_g == base + bb, gate, 0.0), axis=-1, keepdims=True)
                  for bb in range(3)]
            mix = g3[0] * o_c + g3[1] * o_s + g3[2] * o_w
            comb = jnp.where(row8 == h, mix, comb)
        outs.append(comb)
    o_ref[0] = jnp.concatenate(outs, axis=0).astype(o_ref.dtype)
    rolled = pltpu.roll(blk, w_buf - 1, 0)
    rows = lax.broadcasted_iota(jnp.int32, blk.shape, 0)
    nw_ref[0] = jnp.where(rows == w_buf - 1, new, rolled)


def _win_sample(q_nsa, cache_win, new_win, o_c, o_s, g_nsa):
    bsz, w_buf = cache_win.shape[0], cache_win.shape[1]
    kvw = 2 * N_KV * HEAD_DIM
    q4 = q_nsa.reshape(bsz, N_KV, GROUP_R, HEAD_DIM)
    q4 = jnp.pad(q4, ((0, 0), (0, 0), (0, 8 - GROUP_R), (0, 0)))
    o, nw = pl.pallas_call(
        functools.partial(_win_sample_kernel, w_buf=w_buf),
        out_shape=[jax.ShapeDtypeStruct((bsz, 16, HEAD_DIM), BF),
                   jax.ShapeDtypeStruct((bsz, w_buf, kvw), F32)],
        grid=(bsz,),
        in_specs=[pl.BlockSpec((1, N_KV, 8, HEAD_DIM), lambda b: (b, 0, 0, 0)),
                  pl.BlockSpec((1, w_buf, kvw), lambda b: (b, 0, 0)),
                  pl.BlockSpec((1, 1, kvw), lambda b: (b, 0, 0)),
                  pl.BlockSpec((1, N_KV, 8, HEAD_DIM), lambda b: (b, 0, 0, 0)),
                  pl.BlockSpec((1, N_KV, 8, HEAD_DIM), lambda b: (b, 0, 0, 0)),
                  pl.BlockSpec((1, 1, HEAD_DIM), lambda b: (b, 0, 0))],
        out_specs=[pl.BlockSpec((1, 16, HEAD_DIM), lambda b: (b, 0, 0)),
                   pl.BlockSpec((1, w_buf, kvw), lambda b: (b, 0, 0))],
        compiler_params=_cparams(1),
        name="win_sample",
    )(q4, cache_win.reshape(bsz, w_buf, kvw), new_win.reshape(bsz, 1, kvw), o_c, o_s,
      g_nsa.reshape(bsz, 1, HEAD_DIM))
    o = o.reshape(bsz, N_KV, 8, HEAD_DIM)[:, :, :GROUP_R].reshape(bsz, N_H_NSA * HEAD_DIM)
    return o, nw


def _branch_kernel(on_ref, os_ref, wn_ref, ws_ref, mn_ref, ms_ref, o_ref):
    y = (jax.nn.sigmoid(mn_ref[...]) * _dot(on_ref[...], wn_ref[...])
         + jax.nn.sigmoid(ms_ref[...]) * _dot(os_ref[...], ws_ref[...]))
    o_ref[...] = y.astype(o_ref.dtype)


def _branch_mix(o_nsa, o_sb, w_br_nsa, w_br_sb, m_all, tm):
    m, k = o_nsa.shape
    d = w_br_nsa.shape[1]
    tn = 512
    nj = d // tn
    return pl.pallas_call(
        _branch_kernel,
        out_shape=jax.ShapeDtypeStruct((m, d), BF),
        grid=(nj, m // tm),
        in_specs=[pl.BlockSpec((tm, k), lambda j, i: (i, 0)),
                  pl.BlockSpec((tm, k), lambda j, i: (i, 0)),
                  pl.BlockSpec((k, tn), lambda j, i: (0, j)),
                  pl.BlockSpec((k, tn), lambda j, i: (0, j)),
                  pl.BlockSpec((tm, tn), lambda j, i: (i, j)),
                  pl.BlockSpec((tm, tn), lambda j, i: (i, nj + j))],
        out_specs=pl.BlockSpec((tm, tn), lambda j, i: (i, j)),
        compiler_params=_cparams(2),
        name="branch_mix",
    )(o_nsa, o_sb, w_br_nsa, w_br_sb, m_all, m_all)


def _rms(x):
    return x * lax.rsqrt(jnp.mean(x * x, axis=-1, keepdims=True) + EPS)


def _post_mix_kernel(y_ref, w_ref, x_ref, gate_ref, gpost_ref, gpre_ref, sc_ref, sh_ref,
                     x1_ref, h_ref):
    y2 = _dot(y_ref[...], w_ref[...])
    x1 = x_ref[...] + _rows(gate_ref) * (_rms(y2) * gpost_ref[...])
    x1_ref[...] = x1
    h_ref[...] = (_rms(x1) * gpre_ref[...] * (1.0 + _rows(sc_ref))
                  + _rows(sh_ref)).astype(h_ref.dtype)


def _mod_spec2(tm, d, k, row_of_tile):
    if row_of_tile is None:
        return pl.BlockSpec((tm, d), lambda i: (i, k))
    return pl.BlockSpec((1, 1, d), lambda i: (row_of_tile(i), 0, k))


def _post_mix(y, w_out, x2d, mod, g_post_mix, g_pre_ffn, tm, row_of_tile):
    m, d = x2d.shape
    vec = lambda: pl.BlockSpec((1, d), lambda i: (0, 0))
    return pl.pallas_call(
        _post_mix_kernel,
        out_shape=[jax.ShapeDtypeStruct((m, d), F32), jax.ShapeDtypeStruct((m, d), BF)],
        grid=(m // tm,),
        in_specs=[pl.BlockSpec((tm, d), lambda i: (i, 0)),
                  pl.BlockSpec((d, d), lambda i: (0, 0)),
                  pl.BlockSpec((tm, d), lambda i: (i, 0)),
                  _mod_spec2(tm, d, 2, row_of_tile), vec(), vec(),
                  _mod_spec2(tm, d, 4, row_of_tile), _mod_spec2(tm, d, 3, row_of_tile)],
        out_specs=[pl.BlockSpec((tm, d), lambda i: (i, 0)),
                   pl.BlockSpec((tm, d), lambda i: (i, 0))],
        compiler_params=_cparams(1),
        name="post_mix",
    )(y, w_out, x2d, mod, g_post_mix.reshape(1, d), g_pre_ffn.reshape(1, d), mod, mod)


def _ffn_kernel(h_ref, wg_ref, wu_ref, wd_ref, x1_ref, gate_ref, gpost_ref, o_ref, acc_ref):
    j = pl.program_id(1)

    @pl.when(j == 0)
    def _():
        acc_ref[...] = jnp.zeros_like(acc_ref)

    h = h_ref[...]
    a = _dot(h, wg_ref[...])
    act = (a * jax.nn.sigmoid(a) * _dot(h, wu_ref[...])).astype(BF)
    acc_ref[...] += _dot(act, wd_ref[...])

    @pl.when(j == pl.num_programs(1) - 1)
    def _():
        o_ref[...] = x1_ref[...] + _rows(gate_ref) * (_rms(acc_ref[...]) * gpost_ref[...])


def _ffn(h2, wg, wu, wd, x1, mod, g_post_ffn, tm, row_of_tile):
    m, d = x1.shape
    dff = wg.shape[1]
    tf = 512
    if row_of_tile is None:
        gate_spec = pl.BlockSpec((tm, d), lambda i, j: (i, 5))
    else:
        gate_spec = pl.BlockSpec((1, 1, d), lambda i, j: (row_of_tile(i), 0, 5))
    return pl.pallas_call(
        _ffn_kernel,
        out_shape=jax.ShapeDtypeStruct((m, d), F32),
        grid=(m // tm, dff // tf),
        in_specs=[pl.BlockSpec((tm, d), lambda i, j: (i, 0)),
                  pl.BlockSpec((d, tf), lambda i, j: (0, j)),
                  pl.BlockSpec((d, tf), lambda i, j: (0, j)),
                  pl.BlockSpec((tf, d), lambda i, j: (j, 0)),
                  pl.BlockSpec((tm, d), lambda i, j: (i, 0)),
                  gate_spec,
                  pl.BlockSpec((1, d), lambda i, j: (0, 0))],
        out_specs=pl.BlockSpec((tm, d), lambda i, j: (i, 0)),
        scratch_shapes=[pltpu.VMEM((tm, d), F32)],
        compiler_params=_cparams(2),
        name="ffn",
    )(h2, wg, wu, wd, x1, mod, g_post_ffn.reshape(1, d))


def _split_w_in(w_in, d_model):
    q = N_H_NSA * HEAD_DIM
    kv = N_KV * HEAD_DIM
    sb = N_H_SB * HEAD_DIM
    gn = N_H_NSA * 3
    o = 0
    cols = {}
    for name, width in (("q_nsa", q), ("cmp", 2 * kv), ("sel", 2 * kv), ("win", 2 * kv),
                        ("g", gn), ("q_sb", sb), ("kv_sb", 2 * sb), ("m", 2 * d_model)):
        cols[name] = w_in[:, o:o + width].astype(BF)
        o += width
    cols["g"] = jnp.pad(cols["g"], ((0, 0), (0, HEAD_DIM - gn)))
    return cols


def _group(x2d, mod, row_of_tile, tm, pos, w, tail_w):
    g_pre_mix = tail_w["g_pre_mix"]
    h = _prenorm(x2d, g_pre_mix, mod, 1, 0, tm, row_of_tile)
    cos, sin = _rope_tables(pos)
    proj = {}
    proj["q_nsa"], = _project(h, w["q_nsa"], cos, sin, tm, 4, (BF,))
    proj["cmp"], proj["cmp_bf"] = _project(h, w["cmp"], cos, sin, tm, 0, (F32, BF))
    proj["sel"], proj["sel_bf"] = _project(h, w["sel"], cos, sin, tm, N_KV, (F32, BF))
    proj["win"], proj["win_bf"] = _project(h, w["win"], cos, sin, tm, N_KV, (F32, BF))
    proj["g"], = _project(h, w["g"], cos, sin, tm, 0, (F32,))
    proj["q_sb"], = _project(h, w["q_sb"], cos, sin, tm, 0, (BF,))
    proj["kv_sb"], proj["kv_sb_bf"] = _project(h, w["kv_sb"], cos, sin, tm, 0, (F32, BF))
    proj["m"], = _project(h, w["m"], cos, sin, tm, 0, (F32,))
    return proj


def _tail(x2d, o_nsa, o_sb, m_all, mod, row_of_tile, tm, tw):
    y = _branch_mix(o_nsa, o_sb, tw["w_br_nsa"], tw["w_br_sb"], m_all, tm)
    x1, h2 = _post_mix(y, tw["w_out"], x2d, mod, tw["g_post_mix"], tw["g_pre_ffn"], tm,
                       row_of_tile)
    return _ffn(h2, tw["w_ffn_gate"], tw["w_ffn_up"], tw["w_ffn_down"], x1, mod,
                tw["g_post_ffn"], tm, row_of_tile)


def kernel(x_prompt, x_sample, c_prompt, c_sample, cache_cmp, cache_sel, cache_sb, cache_win,
           page_table, w_ada, b_ada, g_pre_mix, g_post_mix, g_pre_ffn, g_post_ffn, w_in,
           pe_ck, w_ck1, b_ck1, w_ck2, pe_cv, w_cv1, b_cv1, w_cv2,
           w_br_nsa, w_br_sb, w_out, w_ffn_gate, w_ffn_up, w_ffn_down):
    n_p, t_p, d = x_prompt.shape
    n_s, t_s, _ = x_sample.shape
    assert t_s == 1
    n_pages = page_table.shape[1]
    past = n_pages * PAGE
    w_buf = cache_win.shape[1]
    kvw = 2 * N_KV * HEAD_DIM

    w = _split_w_in(w_in, d)
    tw = dict(g_pre_mix=g_pre_mix, g_post_mix=g_post_mix, g_pre_ffn=g_pre_ffn,
              g_post_ffn=g_post_ffn, w_br_nsa=w_br_nsa.astype(BF), w_br_sb=w_br_sb.astype(BF),
              w_out=w_out.astype(BF), w_ffn_gate=w_ffn_gate.astype(BF),
              w_ffn_up=w_ffn_up.astype(BF), w_ffn_down=w_ffn_down.astype(BF))
    cmp_w = _cmp_weights(pe_ck, w_ck1, b_ck1, w_ck2, pe_cv, w_cv1, b_cv1, w_cv2)

    n_mod = -(-(n_s + n_p) // 16) * 16
    c_all = jnp.concatenate([c_sample, c_prompt, jnp.zeros((n_mod - n_s - n_p, d), F32)])
    mod = _adaln(c_all, w_ada, b_ada)
    mod3 = mod.reshape(n_mod, 1, 6 * d)

    tm_p = 512
    tiles_per_seq = t_p // tm_p
    row_p = lambda i: n_s + i // tiles_per_seq
    xp = x_prompt.reshape(n_p * t_p, d)
    pp = _group(xp, mod3, row_p, tm_p, jnp.arange(t_p), w, tw)
    kvc = _compress_prompt(pp["cmp"], cmp_w, n_p, t_p)
    tq = 128
    o_nsa = _nsa_prompt(pp["q_nsa"], kvc, pp["sel_bf"], pp["win_bf"], pp["g"], n_p, t_p, tq)
    o_sb = _sb_prompt(pp["q_sb"], pp["kv_sb_bf"], n_p, t_p, tq)
    y_prompt = _tail(xp, o_nsa, o_sb, pp["m"], mod3, row_p, tm_p, tw).reshape(n_p, t_p, d)
    new_cmp_p = pp["cmp"].reshape(n_p, t_p, 2, N_KV, HEAD_DIM)
    new_sel_p = pp["sel"].reshape(n_p, t_p, 2, N_KV, HEAD_DIM)
    new_sb_p = pp["kv_sb"].reshape(n_p, t_p, 2, N_H_SB, HEAD_DIM)
    win_rows = pp["win"].reshape(n_p, t_p, 2, N_KV, HEAD_DIM)
    if t_p >= w_buf:
        new_win_p = win_rows[:, t_p - w_buf:]
    else:
        new_win_p = jnp.pad(win_rows, ((0, 0), (w_buf - t_p, 0), (0, 0), (0, 0), (0, 0)))

    xs = x_sample.reshape(n_s, d)
    pos_s = jnp.full((n_s,), past, jnp.int32)
    ps = _group(xs, mod, None, n_s, pos_s, w, tw)
    o_c, idx = _cmp_sample(ps["q_nsa"], cache_cmp, page_table, cmp_w, past)
    idx = idx[:, :, 0, :min(N_SELECT, -(-(past + 1) // SEL_BLOCK))]
    o_s = _sel_sample(ps["q_nsa"], idx, ps["sel"], cache_sel, page_table)
    o_nsa_s, new_win_s = _win_sample(ps["q_nsa"], cache_win, ps["win"], o_c, o_s, ps["g"])
    o_sb_s = _sb_sample(ps["q_sb"], cache_sb, page_table)
    y_sample = _tail(xs, o_nsa_s, o_sb_s, ps["m"], mod, None, n_s, tw).reshape(n_s, 1, d)

    return (y_prompt, y_sample, new_cmp_p, new_sel_p, new_sb_p, new_win_p,
            ps["cmp"].reshape(n_s, 1, 2, N_KV, HEAD_DIM),
            ps["sel"].reshape(n_s, 1, 2, N_KV, HEAD_DIM),
            ps["kv_sb"].reshape(n_s, 1, 2, N_H_SB, HEAD_DIM),
            new_win_s.reshape(n_s, w_buf, 2, N_KV, HEAD_DIM))
```

```python
import functools

import numpy as np
import jax
import jax.numpy as jnp
from jax import lax
from jax.experimental import pallas as pl
from jax.experimental.pallas import tpu as pltpu

BF = jnp.bfloat16
F32 = jnp.float32

HEAD_DIM = 128
N_KV = 2
GROUP_R = 4
N_H_NSA = N_KV * GROUP_R
N_H_SB = 8
L_CMP = 32
STRIDE = 16
SEL_BLOCK = 64
N_SELECT = 16
WINDOW = 512
PAGE = 128
CMP_HIDDEN = 256
ROPE_THETA = 10000.0
EPS = 1e-6
SCALE = HEAD_DIM ** -0.5
NEG = -1e30
VMEM_LIMIT = 56 * 1024 * 1024


def _cparams(n_axes):
    return pltpu.CompilerParams(dimension_semantics=("arbitrary",) * n_axes,
                                vmem_limit_bytes=VMEM_LIMIT)


def _dot(a, b):
    return jnp.dot(a, b, preferred_element_type=F32)


def _dot_nt(a, b):
    return lax.dot_general(a, b, (((1,), (1,)), ((), ())), preferred_element_type=F32)


def _split3(x):
    hi = x.astype(BF)
    r = x - hi.astype(F32)
    mid = r.astype(BF)
    lo = (r - mid.astype(F32)).astype(BF)
    return hi, mid, lo


def _softmax_rows(s, valid):
    s = jnp.where(valid, s, NEG)
    m = jnp.max(s, axis=-1, keepdims=True)
    m = jnp.where(m > 0.5 * NEG, m, 0.0)
    p = jnp.where(valid, jnp.exp(s - m), 0.0)
    return p / jnp.maximum(jnp.sum(p, axis=-1, keepdims=True), 1e-30)


def _topk_select(val, k, lane):
    big = jnp.int32(1 << 30)
    sel = jnp.zeros(val.shape, F32)
    picks = []
    for _ in range(k):
        m = jnp.max(val, axis=-1, keepdims=True)
        idx = jnp.min(jnp.where(val == m, lane, big), axis=-1, keepdims=True)
        hit = lane == idx
        sel = jnp.where(hit, 1.0, sel)
        val = jnp.where(hit, -2.0, val)
        picks.append(idx)
    return sel, picks


def _block_scores(imp, mmat_ref):
    hi, mid, lo = _split3(imp)
    m = mmat_ref[...]
    return _dot(hi, m) + _dot(mid, m) + _dot(lo, m)


def _rope_tables(pos):
    half = HEAD_DIM // 2
    inv = ROPE_THETA ** (-2.0 * jnp.arange(half, dtype=F32) / HEAD_DIM)
    ang = pos.astype(F32)[:, None] * inv[None, :]
    c, s = jnp.cos(ang), jnp.sin(ang)
    return jnp.concatenate([c, c], axis=1), jnp.concatenate([-s, s], axis=1)


def _rope(x, cos, sin_signed):
    return x * cos + pltpu.roll(x, HEAD_DIM // 2, 1) * sin_signed


def _score_matrix(n_c_pad, n_blk_pad):
    j = np.arange(n_c_pad)[:, None]
    b = np.arange(n_blk_pad)[None, :]
    spb = SEL_BLOCK // STRIDE
    m = ((j == spb * b - 1).astype(np.float32) + (j == spb * b + spb - 1)
         + 2.0 * ((j >= spb * b) & (j <= spb * b + spb - 2)))
    return jnp.asarray(m, BF)


def _ada_kernel(c_ref, w_ref, b_ref, o_ref):
    o_ref[...] = _dot(c_ref[...], w_ref[...].astype(BF)) + b_ref[...]


def _adaln(c_pad, w_ada, b_ada):
    m, d = c_pad.shape
    n = w_ada.shape[1]
    tn = 1536
    return pl.pallas_call(
        _ada_kernel,
        out_shape=jax.ShapeDtypeStruct((m, n), F32),
        grid=(n // tn,),
        in_specs=[pl.BlockSpec((m, d), lambda j: (0, 0)),
                  pl.BlockSpec((d, tn), lambda j: (0, j)),
                  pl.BlockSpec((1, tn), lambda j: (0, j))],
        out_specs=pl.BlockSpec((m, tn), lambda j: (0, j)),
        compiler_params=_cparams(1),
        name="adaln",
    )(c_pad.astype(BF), w_ada, b_ada.reshape(1, n))


def _rows(ref):
    v = ref[...]
    return v.reshape(v.shape[-2], v.shape[-1])


def _prenorm_kernel(x_ref, g_ref, sc_ref, sh_ref, o_ref):
    x = x_ref[...]
    y = x * lax.rsqrt(jnp.mean(x * x, axis=-1, keepdims=True) + EPS)
    o_ref[...] = (y * g_ref[...] * (1.0 + _rows(sc_ref)) + _rows(sh_ref)).astype(o_ref.dtype)


def _mod_spec(tm, d, k, row_of_tile):
    if row_of_tile is None:
        return pl.BlockSpec((tm, d), lambda i: (i, k))
    return pl.BlockSpec((1, 1, d), lambda i: (row_of_tile(i), 0, k))


def _prenorm(x2d, g, mod, k_scale, k_shift, tm, row_of_tile):
    m, d = x2d.shape
    return pl.pallas_call(
        _prenorm_kernel,
        out_shape=jax.ShapeDtypeStruct((m, d), BF),
        grid=(m // tm,),
        in_specs=[pl.BlockSpec((tm, d), lambda i: (i, 0)),
                  pl.BlockSpec((1, d), lambda i: (0, 0)),
                  _mod_spec(tm, d, k_scale, row_of_tile),
                  _mod_spec(tm, d, k_shift, row_of_tile)],
        out_specs=pl.BlockSpec((tm, d), lambda i: (i, 0)),
        compiler_params=_cparams(1),
        name="prenorm",
    )(x2d, g.reshape(1, d), mod, mod)


def _proj_kernel(h_ref, w_ref, cos_ref, sin_ref, *o_refs, rope_heads, tn, cache_rows):
    y = _dot(h_ref[...], w_ref[...])
    n_head = tn // HEAD_DIM
    parts = [y[:, hd * HEAD_DIM:(hd + 1) * HEAD_DIM] for hd in range(n_head)]
    if rope_heads:
        cos, sin = cos_ref[...], sin_ref[...]
        parts = [_rope(p, cos, sin) if hd < rope_heads else p for hd, p in enumerate(parts)]
        y = jnp.concatenate(parts, axis=1)
    if cache_rows:
        tm = y.shape[0]
        for hd, p in enumerate(parts):
            o_refs[0][pl.ds(hd, tm, stride=n_head), :] = p
        o_refs = o_refs[1:]
    for o_ref in o_refs:
        o_ref[...] = y.astype(o_ref.dtype)


def _project(h, w, cos, sin, tm, rope_heads, out_dtypes, cache_rows=False):
    m, d = h.shape
    n = w.shape[1]
    tn = n if cache_rows else min(n, 512)
    t_tiles = cos.shape[0] // tm
    n_head = n // HEAD_DIM
    out_shape = [jax.ShapeDtypeStruct((m, n), dt) for dt in out_dtypes]
    out_specs = [pl.BlockSpec((tm, tn), lambda j, i: (i, j)) for _ in out_dtypes]
    if cache_rows:
        out_shape.insert(0, jax.ShapeDtypeStruct((m * n_head, HEAD_DIM), F32))
        out_specs.insert(0, pl.BlockSpec((tm * n_head, HEAD_DIM), lambda j, i: (i, 0)))
    outs = pl.pallas_call(
        functools.partial(_proj_kernel, rope_heads=rope_heads, tn=tn, cache_rows=cache_rows),
        out_shape=out_shape,
        grid=(n // tn, m // tm),
        in_specs=[pl.BlockSpec((tm, d), lambda j, i: (i, 0)),
                  pl.BlockSpec((d, tn), lambda j, i: (0, j)),
                  pl.BlockSpec((tm, HEAD_DIM), lambda j, i: (i % t_tiles, 0)),
                  pl.BlockSpec((tm, HEAD_DIM), lambda j, i: (i % t_tiles, 0))],
        out_specs=out_specs,
        compiler_params=_cparams(2),
        name="project",
    )(h, w, cos, sin)
    return outs


def _gelu(x):
    return 0.5 * x * (1.0 + jnp.tanh(0.7978845608028654 * (x + 0.044715 * x * x * x)))


def _cmp_bias(pe_ref, w1_ref, b1_ref):
    pb = _dot(pe_ref[0], w1_ref[0])
    return pb[0:1, :CMP_HIDDEN] + pb[1:2, CMP_HIDDEN:] + b1_ref[0]


def _cmp_finish(first, second, bias, w2, cos, sin, is_key, nseg):
    nxt = pltpu.roll(second, nseg - 1, 0)
    hid = _gelu(first + nxt + bias).astype(BF)
    out = _dot(hid, w2)
    out = jnp.where(is_key, _rope(out, cos, sin), out)
    row = lax.broadcasted_iota(jnp.int32, out.shape, 0)
    return jnp.where(row < nseg - 1, out, 0.0)


def _cmp_prompt_kernel(x_ref, w1_ref, pe_ref, b1_ref, w2_ref, cos_ref, sin_ref, o_ref, *, nseg):
    kv = pl.program_id(1)
    n_col = 2 * N_KV
    bias = _cmp_bias(pe_ref, w1_ref, b1_ref)
    for g in range(N_KV):
        segs = []
        for c in (g, N_KV + g):
            segs.append(jnp.concatenate(
                [x_ref[0, pl.ds(r * n_col + c, nseg, stride=STRIDE * n_col), :]
                 for r in range(STRIDE)], axis=1))
        seg = jnp.where(kv == 0, segs[0], segs[1]).astype(BF)
        fs = _dot(seg, w1_ref[0])
        out = _cmp_finish(fs[:, :CMP_HIDDEN], fs[:, CMP_HIDDEN:], bias, w2_ref[0],
                          cos_ref[...], sin_ref[...], kv == 0, nseg)
        o_ref[0, g] = out.astype(o_ref.dtype)


def _cmp_weights(pe_ck, w_ck1, b_ck1, w_ck2, pe_cv, w_cv1, b_cv1, w_cv2):
    half = STRIDE * HEAD_DIM

    def w1cat(w):
        return jnp.concatenate([w[:half], w[half:]], axis=1)

    w1 = jnp.stack([w1cat(w_ck1), w1cat(w_cv1)]).astype(BF)
    pe = jnp.stack([pe_ck.reshape(2, half), pe_cv.reshape(2, half)])
    pe = jnp.pad(pe, ((0, 0), (0, 6), (0, 0))).astype(BF)
    b1 = jnp.stack([b_ck1, b_cv1]).reshape(2, 1, CMP_HIDDEN)
    w2 = jnp.stack([w_ck2, w_cv2]).astype(BF)
    return w1, pe, b1, w2


def _compress_prompt(kv_cmp, cmp_w, n, t):
    w1, pe, b1, w2 = cmp_w
    nseg = t // STRIDE
    n_col = 2 * N_KV
    cos, sin = _rope_tables(jnp.arange(nseg) * STRIDE)
    x3 = kv_cmp.reshape(n, t * n_col, HEAD_DIM)
    return pl.pallas_call(
        functools.partial(_cmp_prompt_kernel, nseg=nseg),
        out_shape=jax.ShapeDtypeStruct((n, n_col, nseg, HEAD_DIM), BF),
        grid=(n, 2),
        in_specs=[pl.BlockSpec((1, t * n_col, HEAD_DIM), lambda b, kv: (b, 0, 0)),
                  pl.BlockSpec((1,) + w1.shape[1:], lambda b, kv: (kv, 0, 0)),
                  pl.BlockSpec((1,) + pe.shape[1:], lambda b, kv: (kv, 0, 0)),
                  pl.BlockSpec((1,) + b1.shape[1:], lambda b, kv: (kv, 0, 0)),
                  pl.BlockSpec((1,) + w2.shape[1:], lambda b, kv: (kv, 0, 0)),
                  pl.BlockSpec((nseg, HEAD_DIM), lambda b, kv: (0, 0)),
                  pl.BlockSpec((nseg, HEAD_DIM), lambda b, kv: (0, 0))],
        out_specs=pl.BlockSpec((1, N_KV, nseg, HEAD_DIM), lambda b, kv: (b, kv, 0, 0)),
        compiler_params=_cparams(2),
        name="compress_prompt",
    )(x3, w1, pe, b1, w2, cos, sin)


def _flash_step(qs, k, v, valid, m, l, acc):
    tq, tk = valid.shape
    s = (_dot_nt(qs, k) * SCALE).reshape(GROUP_R, tq, tk)
    s = jnp.where(valid[None], s, NEG).reshape(GROUP_R * tq, tk)
    m_new = jnp.maximum(m, jnp.max(s, axis=-1, keepdims=True))
    alpha = jnp.exp(m - m_new)
    p = jnp.exp(s - m_new).reshape(GROUP_R, tq, tk)
    p = jnp.where(valid[None], p, 0.0).reshape(GROUP_R * tq, tk)
    l = alpha * l + jnp.sum(p, axis=-1, keepdims=True)
    acc = alpha * acc + _dot(p.astype(BF), v)
    return m_new, l, acc


def _nsa_prompt_kernel(q_ref, kc_ref, vc_ref, ks_ref, vs_ref, kw_ref, vw_ref, g_ref, mm_ref,
                       o_ref, *, tq, n_c, n_blk):
    g_idx = pl.program_id(1)
    qi = pl.program_id(2)
    q = q_ref[...]
    qs = jnp.concatenate([q[:, h * HEAD_DIM:(h + 1) * HEAD_DIM] for h in range(GROUP_R)], axis=0)
    rows = GROUP_R * tq
    qpos = qi * tq + lax.broadcasted_iota(jnp.int32, (tq, 1), 0)

    n_cp = kc_ref.shape[2]
    s = (_dot_nt(qs, kc_ref[0, 0]) * SCALE).reshape(GROUP_R, tq, n_cp)
    jc = lax.broadcasted_iota(jnp.int32, (tq, n_cp), 1)
    valid_c = (jc * STRIDE + (L_CMP - 1) <= qpos) & (jc < n_c)
    p_c = _softmax_rows(s, valid_c[None])
    o_c = _dot(p_c.reshape(rows, n_cp).astype(BF), vc_ref[0, 0])
    imp = jnp.sum(p_c, axis=0)

    score = _block_scores(imp, mm_ref)
    n_lane = score.shape[1]
    blk = lax.broadcasted_iota(jnp.int32, (tq, n_lane), 1)
    cur = qpos // SEL_BLOCK
    forced = (blk == 0) | (blk == cur) | (blk == cur - 1)
    val = jnp.where((blk > cur) | (blk >= n_blk), -1.0, jnp.where(forced, 3e38, score))
    sel, _ = _topk_select(val, min(N_SELECT, n_blk), blk)
    sel = sel.astype(BF)

    lane_k = lax.broadcasted_iota(jnp.int32, (tq, tq), 1)
    eb = lax.broadcasted_iota(jnp.int32, (n_lane, tq), 0)
    ej = lax.broadcasted_iota(jnp.int32, (n_lane, tq), 1) // SEL_BLOCK
    init = (jnp.full((rows, 1), NEG, F32), jnp.zeros((rows, 1), F32),
            jnp.zeros((rows, HEAD_DIM), F32))

    def sel_body(kt, carry):
        k0 = pl.multiple_of(kt * tq, tq)
        expand = jnp.where(eb == ej + kt * (tq // SEL_BLOCK), 1.0, 0.0).astype(BF)
        chosen = _dot(sel, expand) > 0.5
        valid = chosen & (k0 + lane_k <= qpos)
        return _flash_step(qs, ks_ref[pl.ds(k0, tq), :], vs_ref[pl.ds(k0, tq), :], valid, *carry)

    _, l_s, acc_s = lax.fori_loop(0, qi + 1, sel_body, init)
    o_s = acc_s / jnp.maximum(l_s, 1e-30)

    def win_body(kt, carry):
        k0 = pl.multiple_of(kt * tq, tq)
        dist = qpos - (k0 + lane_k)
        valid = (dist >= 0) & (dist < WINDOW)
        return _flash_step(qs, kw_ref[pl.ds(k0, tq), :], vw_ref[pl.ds(k0, tq), :], valid, *carry)

    _, l_w, acc_w = lax.fori_loop(jnp.maximum(qi - WINDOW // tq, 0), qi + 1, win_body, init)
    o_w = acc_w / jnp.maximum(l_w, 1e-30)

    gate = jax.nn.sigmoid(g_ref[...])
    lane_g = lax.broadcasted_iota(jnp.int32, gate.shape, 1)
    outs = []
    for h in range(GROUP_R):
        base = (g_idx * GROUP_R + h) * 3
        g3 = [jnp.sum(jnp.where(lane_g == base + b, gate, 0.0), axis=-1, keepdims=True)
              for b in range(3)]
        sl = slice(h * tq, (h + 1) * tq)
        outs.append(g3[0] * o_c[sl] + g3[1] * o_s[sl] + g3[2] * o_w[sl])
    o_ref[...] = jnp.concatenate(outs, axis=1).astype(o_ref.dtype)


def _nsa_prompt(q_nsa, kvc, sel_bf, win_bf, g_nsa, n, t, tq):
    nseg = kvc.shape[2]
    n_c = nseg - 1
    n_blk = -(-t // SEL_BLOCK)
    qt = t // tq
    gw = GROUP_R * HEAD_DIM
    mmat = _score_matrix(nseg, 128)
    kv_spec = lambda col: pl.BlockSpec((t, HEAD_DIM), lambda b, g, i: (b, col + g))
    return pl.pallas_call(
        functools.partial(_nsa_prompt_kernel, tq=tq, n_c=n_c, n_blk=n_blk),
        out_shape=jax.ShapeDtypeStruct((n * t, N_H_NSA * HEAD_DIM), BF),
        grid=(n, N_KV, qt),
        in_specs=[pl.BlockSpec((tq, gw), lambda b, g, i: (b * qt + i, g)),
                  pl.BlockSpec((1, 1, nseg, HEAD_DIM), lambda b, g, i: (b, g, 0, 0)),
                  pl.BlockSpec((1, 1, nseg, HEAD_DIM), lambda b, g, i: (b, N_KV + g, 0, 0)),
                  kv_spec(0), kv_spec(N_KV), kv_spec(0), kv_spec(N_KV),
                  pl.BlockSpec((tq, HEAD_DIM), lambda b, g, i: (b * qt + i, 0)),
                  pl.BlockSpec(mmat.shape, lambda b, g, i: (0, 0))],
        out_specs=pl.BlockSpec((tq, gw), lambda b, g, i: (b * qt + i, g)),
        compiler_params=_cparams(3),
        name="nsa_prompt",
    )(q_nsa, kvc, kvc, sel_bf, sel_bf, win_bf, win_bf, g_nsa, mmat)


def _softplus(z):
    return jnp.maximum(z, 0.0) + jnp.log1p(jnp.exp(-jnp.abs(z)))


def _strict_upper(n):
    return jnp.where(lax.broadcasted_iota(jnp.int32, (n, n), 0)
                     > lax.broadcasted_iota(jnp.int32, (n, n), 1), 1.0, 0.0).astype(BF)


def _suffix_sum(lm, u):
    hi = lm.astype(BF)
    lo = (lm - hi.astype(F32)).astype(BF)
    return _dot(hi, u) + _dot(lo, u)


SB_DEAD = -110.0


def _sb_tile(q, k, v, u, causal, run, acc):
    z = _dot_nt(q, k) * SCALE
    sp = _softplus(z)
    lm = -sp if causal is None else jnp.where(causal, -sp, 0.0)
    a = jnp.exp(z - sp + _suffix_sum(lm, u) + run)
    if causal is not None:
        a = jnp.where(causal, a, 0.0)
    acc = acc + _dot(a.astype(BF), v)
    return run + jnp.sum(lm, axis=-1, keepdims=True), acc


def _sb_prompt_kernel(q_ref, k_ref, v_ref, o_ref, *, tq, nh):
    qi = pl.program_id(2)
    u = _strict_upper(tq)
    qpos = qi * tq + lax.broadcasted_iota(jnp.int32, (tq, 1), 0)
    lane = lax.broadcasted_iota(jnp.int32, (tq, tq), 1)

    def cond(st):
        return (st[0] <= qi) & (st[1] > 0)

    def body(st):
        step = st[0]
        k0 = pl.multiple_of((qi - step) * tq, tq)
        causal = k0 + lane < qpos
        runs, accs, top = [], [], None
        for h in range(nh):
            cols = slice(h * HEAD_DIM, (h + 1) * HEAD_DIM)
            run, acc = _sb_tile(q_ref[:, cols], k_ref[pl.ds(k0, tq), cols],
                                v_ref[pl.ds(k0, tq), cols], u, causal, st[2 + h], st[2 + nh + h])
            runs.append(run)
            accs.append(acc)
            top = jnp.max(run) if top is None else jnp.maximum(top, jnp.max(run))
        return (step + 1, (top > SB_DEAD).astype(jnp.int32), *runs, *accs)

    init = ((jnp.int32(0), jnp.int32(1)) + tuple(jnp.zeros((tq, 1), F32) for _ in range(nh))
            + tuple(jnp.zeros((tq, HEAD_DIM), F32) for _ in range(nh)))
    st = lax.while_loop(cond, body, init)
    o_ref[...] = jnp.concatenate(st[2 + nh:], axis=1).astype(o_ref.dtype)


def _sb_prompt(q_sb, kv_sb_bf, n, t, tq):
    qt = t // tq
    nh = 4
    ng = N_H_SB // nh
    return pl.pallas_call(
        functools.partial(_sb_prompt_kernel, tq=tq, nh=nh),
        out_shape=jax.ShapeDtypeStruct((n * t, N_H_SB * HEAD_DIM), BF),
        grid=(n, ng, qt),
        in_specs=[pl.BlockSpec((tq, nh * HEAD_DIM), lambda b, h, i: (b * qt + i, h)),
                  pl.BlockSpec((t, nh * HEAD_DIM), lambda b, h, i: (b, h)),
                  pl.BlockSpec((t, nh * HEAD_DIM), lambda b, h, i: (b, ng + h))],
        out_specs=pl.BlockSpec((tq, nh * HEAD_DIM), lambda b, h, i: (b * qt + i, h)),
        compiler_params=_cparams(3),
        name="sb_prompt",
    )(q_sb, kv_sb_bf, kv_sb_bf)


def _sb_sample_kernel(pt_ref, q_ref, cache_ref, o_ref, buf, sem, *, n_pg, n_pages):
    b = pl.program_id(0)
    n_chunk = n_pages // n_pg
    rows_per_tok = 2 * N_H_SB

    def page_copy(chunk, slot, pg):
        page = pt_ref[b * n_pages + (n_chunk - 1 - chunk) * n_pg + pg]
        return pltpu.make_async_copy(cache_ref.at[page], buf.at[slot, pg], sem.at[slot, pg])

    def start(chunk, slot):
        for pg in range(n_pg):
            page_copy(chunk, slot, pg).start()

    def wait(chunk, slot):
        for pg in range(n_pg):
            page_copy(chunk, slot, pg).wait()

    start(0, 0)
    q = q_ref[0]
    u = _strict_upper(PAGE)
    row = lax.broadcasted_iota(jnp.int32, (N_H_SB, PAGE), 0)

    def cond(st):
        return (st[0] < n_chunk) & (st[1] > 0)

    def body(st):
        c, _, run, acc = st
        slot = c % 2
        wait(c, slot)

        @pl.when(c + 1 < n_chunk)
        def _():
            start(c + 1, 1 - slot)

        for pg in reversed(range(n_pg)):
            page = buf.at[slot, pg]
            z = jnp.zeros((N_H_SB, PAGE), F32)
            for h in range(N_H_SB):
                kh = page[pl.ds(h, PAGE, stride=rows_per_tok), :].astype(BF)
                z = jnp.where(row == h, _dot_nt(q, kh), z)
            z = z * SCALE
            sp = _softplus(z)
            lm = -sp
            a = jnp.exp(z - sp + _suffix_sum(lm, u) + run).astype(BF)
            for h in range(N_H_SB):
                vh = page[pl.ds(N_H_SB + h, PAGE, stride=rows_per_tok), :].astype(BF)
                acc = acc + jnp.where(row == h, _dot(a, vh), 0.0)
            run = run + jnp.sum(lm, axis=-1, keepdims=True)
        return c + 1, (jnp.max(run) > SB_DEAD).astype(jnp.int32), run, acc

    c, _, _, acc = lax.while_loop(
        cond, body, (jnp.int32(0), jnp.int32(1), jnp.zeros((N_H_SB, 1), F32),
                     jnp.zeros((N_H_SB, HEAD_DIM), F32)))

    @pl.when(c < n_chunk)
    def _():
        wait(c, c % 2)

    o_ref[0] = acc.astype(o_ref.dtype)


def _sb_sample(q_sb, cache_sb, page_table):
    bsz, n_pages = page_table.shape
    n_phys = cache_sb.shape[0]
    n_pg = 2
    hw = N_H_SB * HEAD_DIM
    page_rows = PAGE * 2 * N_H_SB
    cache = cache_sb.reshape(n_phys, page_rows, HEAD_DIM)
    out = pl.pallas_call(
        functools.partial(_sb_sample_kernel, n_pg=n_pg, n_pages=n_pages),
        out_shape=jax.ShapeDtypeStruct((bsz, N_H_SB, HEAD_DIM), BF),
        grid_spec=pltpu.PrefetchScalarGridSpec(
            num_scalar_prefetch=1,
            grid=(bsz,),
            in_specs=[pl.BlockSpec((1, N_H_SB, HEAD_DIM), lambda b, pt: (b, 0, 0)),
                      pl.BlockSpec(memory_space=pl.ANY)],
            out_specs=pl.BlockSpec((1, N_H_SB, HEAD_DIM), lambda b, pt: (b, 0, 0)),
            scratch_shapes=[pltpu.VMEM((2, n_pg, page_rows, HEAD_DIM), F32),
                            pltpu.SemaphoreType.DMA((2, n_pg))]),
        compiler_params=_cparams(1),
        name="sb_sample",
    )(page_table.reshape(-1), q_sb.reshape(bsz, N_H_SB, HEAD_DIM), cache)
    return out.reshape(bsz, hw)


def _cmp_sample_kernel(pt_ref, q_ref, w1_ref, pe_ref, b1_ref, w2_ref, cos_ref, sin_ref, mm_ref,
                       *refs, n_pg, nseg, n_blk, q_pos):
    n_col = 2 * N_KV
    page_refs = refs[:n_pg]
    oc_ref, idx_ref = refs[n_pg:n_pg + 2]
    fs_ref = refs[n_pg + 2]
    j = pl.program_id(1)
    spp = PAGE // STRIDE
    rows = n_pg * spp
    row0 = pl.multiple_of(j * rows, rows)
    for kv in range(2):
        segs = []
        for c in range(kv * N_KV, (kv + 1) * N_KV):
            for pg in range(n_pg):
                segs.append(jnp.concatenate(
                    [page_refs[pg][0, pl.ds(r * n_col + c, spp, stride=STRIDE * n_col), :]
                     for r in range(STRIDE)], axis=1))
        segs = jnp.concatenate(segs, axis=0).astype(BF)
        prod = _dot(segs, w1_ref[kv])
        for g in range(N_KV):
            fs_ref[kv * N_KV + g, pl.ds(row0, rows), :] = prod[g * rows:(g + 1) * rows]

    @pl.when(j == pl.num_programs(1) - 1)
    def _():
        n_c = nseg - 1
        cos, sin = cos_ref[...], sin_ref[...]
        comp = []
        for c in range(2 * N_KV):
            kv = c // N_KV
            pb = _dot(pe_ref[kv], w1_ref[kv])
            bias = pb[0:1, :CMP_HIDDEN] + pb[1:2, CMP_HIDDEN:] + b1_ref[kv]
            fs = fs_ref[c]
            comp.append(_cmp_finish(fs[:, :CMP_HIDDEN], fs[:, CMP_HIDDEN:], bias, w2_ref[kv],
                                    cos, sin, kv == 0, nseg).astype(BF))
        jc = lax.broadcasted_iota(jnp.int32, (8, nseg), 1)
        valid = (jc * STRIDE + (L_CMP - 1) <= q_pos) & (jc < n_c)
        head = lax.broadcasted_iota(jnp.int32, (8, nseg), 0)
        n_lane = mm_ref.shape[1]
        blk = lax.broadcasted_iota(jnp.int32, (8, n_lane), 1)
        cur = q_pos // SEL_BLOCK
        forced = (blk == 0) | (blk == cur) | (blk == cur - 1)
        lane_o = lax.broadcasted_iota(jnp.int32, (8, HEAD_DIM), 1)
        for g in range(N_KV):
            qg = q_ref[0, g]
            s = _dot_nt(qg, comp[g]) * SCALE
            p = _softmax_rows(s, valid)
            oc_ref[0, g] = _dot(p.astype(BF), comp[N_KV + g])
            imp = jnp.sum(jnp.where(head < GROUP_R, p, 0.0), axis=0, keepdims=True)
            score = _block_scores(jnp.broadcast_to(imp, (8, nseg)), mm_ref)
            val = jnp.where((blk > cur) | (blk >= n_blk), -1.0, jnp.where(forced, 3e38, score))
            _, picks = _topk_select(val, min(N_SELECT, n_blk), blk)
            row = jnp.zeros((8, HEAD_DIM), jnp.int32)
            for kk, pick in enumerate(picks):
                row = jnp.where(lane_o == kk, pick, row)
            idx_ref[0, g] = row


def _cmp_sample(q_nsa, cache_cmp, page_table, cmp_w, q_pos):
    w1, pe, b1, w2 = cmp_w
    bsz, n_pages = page_table.shape
    n_phys = cache_cmp.shape[0]
    n_pg = min(32, n_pages)
    n_chunk = n_pages // n_pg
    nseg = n_pages * PAGE // STRIDE
    n_blk = -(-(n_pages * PAGE + 1) // SEL_BLOCK)
    n_lane = -(-n_blk // 128) * 128
    mmat = _score_matrix(nseg, n_lane)
    cos, sin = _rope_tables(jnp.arange(nseg) * STRIDE)
    page_rows = PAGE * 2 * N_KV
    cache = cache_cmp.reshape(n_phys, page_rows, HEAD_DIM)
    q4 = q_nsa.reshape(bsz, N_KV, GROUP_R, HEAD_DIM)
    q4 = jnp.pad(q4, ((0, 0), (0, 0), (0, 8 - GROUP_R), (0, 0)))

    def page_spec(pg):
        return pl.BlockSpec((1, page_rows, HEAD_DIM),
                            lambda b, j, pt: (pt[b * n_pages + j * n_pg + pg], 0, 0))

    page_specs = [page_spec(pg) for pg in range(n_pg)]
    full = lambda a: pl.BlockSpec(a.shape, lambda b, j, pt: (0,) * a.ndim)
    return pl.pallas_call(
        functools.partial(_cmp_sample_kernel, n_pg=n_pg, nseg=nseg, n_blk=n_blk, q_pos=q_pos),
        out_shape=[jax.ShapeDtypeStruct((bsz, N_KV, 8, HEAD_DIM), F32),
                   jax.ShapeDtypeStruct((bsz, N_KV, 8, HEAD_DIM), jnp.int32)],
        grid_spec=pltpu.PrefetchScalarGridSpec(
            num_scalar_prefetch=1,
            grid=(bsz, n_chunk),
            in_specs=[pl.BlockSpec((1, N_KV, 8, HEAD_DIM), lambda b, j, pt: (b, 0, 0, 0)),
                      full(w1), full(pe), full(b1), full(w2), full(cos), full(sin), full(mmat)]
            + page_specs,
            out_specs=[pl.BlockSpec((1, N_KV, 8, HEAD_DIM), lambda b, j, pt: (b, 0, 0, 0)),
                       pl.BlockSpec((1, N_KV, 8, HEAD_DIM), lambda b, j, pt: (b, 0, 0, 0))],
            scratch_shapes=[pltpu.VMEM((2 * N_KV, nseg, 2 * CMP_HIDDEN), F32)]),
        compiler_params=_cparams(2),
        name="cmp_sample",
    )(page_table.reshape(-1), q4, w1, pe, b1, w2, cos, sin, mmat,
      *([cache] * len(page_specs)))


def _sel_sample_kernel(idx_ref, pt_ref, q_ref, new_ref, *refs, n_sel, last_blk):
    o_ref = refs[N_KV * n_sel]
    b = pl.program_id(0)
    n_col = 2 * N_KV
    n_key = n_sel * SEL_BLOCK
    slot = lax.broadcasted_iota(jnp.int32, (8, n_key), 1) // SEL_BLOCK
    new = new_ref[0]
    for g in range(N_KV):
        blks = refs[g * n_sel:(g + 1) * n_sel]
        q = q_ref[0, g]
        keys = jnp.concatenate(
            [r[0, pl.ds(g, SEL_BLOCK, stride=n_col), :] for r in blks], axis=0).astype(BF)
        vals = jnp.concatenate(
            [r[0, pl.ds(N_KV + g, SEL_BLOCK, stride=n_col), :] for r in blks], axis=0).astype(BF)
        s = _dot_nt(q, keys) * SCALE
        valid = jnp.zeros((8, n_key), jnp.bool_)
        has_new = jnp.zeros((8, 1), jnp.bool_)
        for kk in range(n_sel):
            bi = idx_ref[(b * N_KV + g) * n_sel + kk]
            valid = valid | ((slot == kk) & (bi < last_blk))
            has_new = has_new | (bi == last_blk)
        k_new = new[g:g + 1].astype(BF).astype(F32)
        v_new = new[N_KV + g:N_KV + g + 1].astype(BF).astype(F32)
        s_new = jnp.sum(q.astype(F32) * k_new, axis=-1, keepdims=True) * SCALE
        s = jnp.where(valid, s, NEG)
        s_new = jnp.where(has_new, s_new, NEG)
        m = jnp.maximum(jnp.max(s, axis=-1, keepdims=True), s_new)
        m = jnp.where(m > 0.5 * NEG, m, 0.0)
        p = jnp.where(valid, jnp.exp(s - m), 0.0)
        p_new = jnp.where(has_new, jnp.exp(s_new - m), 0.0)
        den = jnp.maximum(jnp.sum(p, axis=-1, keepdims=True) + p_new, 1e-30)
        p = p / den
        p_new = p_new / den
        o_ref[0, g] = _dot(p.astype(BF), vals) + p_new.astype(BF).astype(F32) * v_new


def _sel_sample(q_nsa, idx, new_sel, cache_sel, page_table):
    bsz, n_pages = page_table.shape
    n_phys = cache_sel.shape[0]
    n_sel = idx.shape[2]
    halves = PAGE // SEL_BLOCK
    last_blk = n_pages * halves
    blk_rows = SEL_BLOCK * 2 * N_KV
    cache = cache_sel.reshape(n_phys * halves, blk_rows, HEAD_DIM)
    q4 = q_nsa.reshape(bsz, N_KV, GROUP_R, HEAD_DIM)
    q4 = jnp.pad(q4, ((0, 0), (0, 0), (0, 8 - GROUP_R), (0, 0)))
    new4 = new_sel.reshape(bsz, 2 * N_KV, HEAD_DIM)

    def blk_spec(g, kk):
        def imap(b, idx_r, pt_r):
            bi = jnp.minimum(idx_r[(b * N_KV + g) * n_sel + kk], last_blk - 1)
            return (pt_r[b * n_pages + bi // halves] * halves + bi % halves, 0, 0)
        return pl.BlockSpec((1, blk_rows, HEAD_DIM), imap)

    return pl.pallas_call(
        functools.partial(_sel_sample_kernel, n_sel=n_sel, last_blk=last_blk),
        out_shape=jax.ShapeDtypeStruct((bsz, N_KV, 8, HEAD_DIM), F32),
        grid_spec=pltpu.PrefetchScalarGridSpec(
            num_scalar_prefetch=2,
            grid=(bsz,),
            in_specs=[pl.BlockSpec((1, N_KV, 8, HEAD_DIM), lambda b, i_r, p_r: (b, 0, 0, 0)),
                      pl.BlockSpec((1, 2 * N_KV, HEAD_DIM), lambda b, i_r, p_r: (b, 0, 0))]
            + [blk_spec(g, kk) for g in range(N_KV) for kk in range(n_sel)],
            out_specs=pl.BlockSpec((1, N_KV, 8, HEAD_DIM), lambda b, i_r, p_r: (b, 0, 0, 0))),
        compiler_params=_cparams(1),
        name="sel_sample",
    )(idx.reshape(-1), page_table.reshape(-1), q4, new4, *([cache] * (N_KV * n_sel)))


def _win_sample_kernel(q_ref, win_ref, new_ref, oc_ref, os_ref, g_ref, o_ref, nw_ref, *, w_buf):
    n_col = 2 * N_KV
    new = new_ref[0]
    i = lax.broadcasted_iota(jnp.int32, (8, w_buf), 1)
    valid = (w_buf - i >= 0) & (w_buf - i < WINDOW)
    gate = jax.nn.sigmoid(g_ref[0])
    lane_g = lax.broadcasted_iota(jnp.int32, gate.shape, 1)
    row8 = lax.broadcasted_iota(jnp.int32, (8, HEAD_DIM), 0)
    outs = []
    for g in range(N_KV):
        q = q_ref[0, g]
        k = win_ref[0, pl.ds(g, w_buf, stride=n_col), :].astype(BF)
        v = win_ref[0, pl.ds(N_KV + g, w_buf, stride=n_col), :].astype(BF)
        k_new = new[g:g + 1].astype(BF).astype(F32)
        v_new = new[N_KV + g:N_KV + g + 1].astype(BF).astype(F32)
        s = jnp.where(valid, _dot_nt(q, k) * SCALE, NEG)
        s_new = jnp.sum(q.astype(F32) * k_new, axis=-1, keepdims=True) * SCALE
        m = jnp.maximum(jnp.max(s, axis=-1, keepdims=True), s_new)
        p = jnp.where(valid, jnp.exp(s - m), 0.0)
        p_new = jnp.exp(s_new - m)
        den = jnp.sum(p, axis=-1, keepdims=True) + p_new
        p = p / den
        p_new = p_new / den
        o_w = _dot(p.astype(BF), v) + p_new.astype(BF).astype(F32) * v_new
        o_c, o_s = oc_ref[0, g], os_ref[0, g]
        comb = jnp.zeros((8, HEAD_DIM), F32)
        for h in range(GROUP_R):
            base = (g * GROUP_R + h) * 3
            g3 = [jnp.sum(jnp.where(lane_g == base + bb, gate, 0.0), axis=-1, keepdims=True)
                  for bb in range(3)]
            mix = g3[0] * o_c + g3[1] * o_s + g3[2] * o_w
            comb = jnp.where(row8 == h, mix, comb)
        outs.append(comb)
    o_ref[0] = jnp.concatenate(outs, axis=0).astype(o_ref.dtype)
    n_rows = w_buf * n_col
    nw_ref[0] = pltpu.roll(win_ref[0], n_rows - n_col, 0)
    nw_ref[0, n_rows - n_col:, :] = new


def _win_sample(q_nsa, cache_win, new_win, o_c, o_s, g_nsa):
    bsz, w_buf = cache_win.shape[0], cache_win.shape[1]
    n_col = 2 * N_KV
    q4 = q_nsa.reshape(bsz, N_KV, GROUP_R, HEAD_DIM)
    q4 = jnp.pad(q4, ((0, 0), (0, 0), (0, 8 - GROUP_R), (0, 0)))
    o, nw = pl.pallas_call(
        functools.partial(_win_sample_kernel, w_buf=w_buf),
        out_shape=[jax.ShapeDtypeStruct((bsz, 16, HEAD_DIM), BF),
                   jax.ShapeDtypeStruct((bsz, w_buf * n_col, HEAD_DIM), F32)],
        grid=(bsz,),
        in_specs=[pl.BlockSpec((1, N_KV, 8, HEAD_DIM), lambda b: (b, 0, 0, 0)),
                  pl.BlockSpec((1, w_buf * n_col, HEAD_DIM), lambda b: (b, 0, 0)),
                  pl.BlockSpec((1, n_col, HEAD_DIM), lambda b: (b, 0, 0)),
                  pl.BlockSpec((1, N_KV, 8, HEAD_DIM), lambda b: (b, 0, 0, 0)),
                  pl.BlockSpec((1, N_KV, 8, HEAD_DIM), lambda b: (b, 0, 0, 0)),
                  pl.BlockSpec((1, 1, HEAD_DIM), lambda b: (b, 0, 0))],
        out_specs=[pl.BlockSpec((1, 16, HEAD_DIM), lambda b: (b, 0, 0)),
                   pl.BlockSpec((1, w_buf * n_col, HEAD_DIM), lambda b: (b, 0, 0))],
        compiler_params=_cparams(1),
        name="win_sample",
    )(q4, cache_win.reshape(bsz, w_buf * n_col, HEAD_DIM), new_win.reshape(bsz, n_col, HEAD_DIM),
      o_c, o_s, g_nsa.reshape(bsz, 1, HEAD_DIM))
    o = o.reshape(bsz, N_KV, 8, HEAD_DIM)[:, :, :GROUP_R].reshape(bsz, N_H_NSA * HEAD_DIM)
    return o, nw


def _branch_kernel(on_ref, os_ref, wn_ref, ws_ref, mn_ref, ms_ref, o_ref):
    y = (jax.nn.sigmoid(mn_ref[...]) * _dot(on_ref[...], wn_ref[...])
         + jax.nn.sigmoid(ms_ref[...]) * _dot(os_ref[...], ws_ref[...]))
    o_ref[...] = y.astype(o_ref.dtype)


def _branch_mix(o_nsa, o_sb, w_br_nsa, w_br_sb, m_all, tm):
    m, k = o_nsa.shape
    d = w_br_nsa.shape[1]
    tn = 512
    nj = d // tn
    return pl.pallas_call(
        _branch_kernel,
        out_shape=jax.ShapeDtypeStruct((m, d), BF),
        grid=(nj, m // tm),
        in_specs=[pl.BlockSpec((tm, k), lambda j, i: (i, 0)),
                  pl.BlockSpec((tm, k), lambda j, i: (i, 0)),
                  pl.BlockSpec((k, tn), lambda j, i: (0, j)),
                  pl.BlockSpec((k, tn), lambda j, i: (0, j)),
                  pl.BlockSpec((tm, tn), lambda j, i: (i, j)),
                  pl.BlockSpec((tm, tn), lambda j, i: (i, nj + j))],
        out_specs=pl.BlockSpec((tm, tn), lambda j, i: (i, j)),
        compiler_params=_cparams(2),
        name="branch_mix",
    )(o_nsa, o_sb, w_br_nsa, w_br_sb, m_all, m_all)


def _rms(x):
    return x * lax.rsqrt(jnp.mean(x * x, axis=-1, keepdims=True) + EPS)


def _post_mix_kernel(y_ref, w_ref, x_ref, gate_ref, gpost_ref, gpre_ref, sc_ref, sh_ref,
                     x1_ref, h_ref):
    y2 = _dot(y_ref[...], w_ref[...])
    x1 = x_ref[...] + _rows(gate_ref) * (_rms(y2) * gpost_ref[...])
    x1_ref[...] = x1
    h_ref[...] = (_rms(x1) * gpre_ref[...] * (1.0 + _rows(sc_ref))
                  + _rows(sh_ref)).astype(h_ref.dtype)


def _mod_spec2(tm, d, k, row_of_tile):
    if row_of_tile is None:
        return pl.BlockSpec((tm, d), lambda i: (i, k))
    return pl.BlockSpec((1, 1, d), lambda i: (row_of_tile(i), 0, k))


def _post_mix(y, w_out, x2d, mod, g_post_mix, g_pre_ffn, tm, row_of_tile):
    m, d = x2d.shape
    vec = lambda: pl.BlockSpec((1, d), lambda i: (0, 0))
    return pl.pallas_call(
        _post_mix_kernel,
        out_shape=[jax.ShapeDtypeStruct((m, d), F32), jax.ShapeDtypeStruct((m, d), BF)],
        grid=(m // tm,),
        in_specs=[pl.BlockSpec((tm, d), lambda i: (i, 0)),
                  pl.BlockSpec((d, d), lambda i: (0, 0)),
                  pl.BlockSpec((tm, d), lambda i: (i, 0)),
                  _mod_spec2(tm, d, 2, row_of_tile), vec(), vec(),
                  _mod_spec2(tm, d, 4, row_of_tile), _mod_spec2(tm, d, 3, row_of_tile)],
        out_specs=[pl.BlockSpec((tm, d), lambda i: (i, 0)),
                   pl.BlockSpec((tm, d), lambda i: (i, 0))],
        compiler_params=_cparams(1),
        name="post_mix",
    )(y, w_out, x2d, mod, g_post_mix.reshape(1, d), g_pre_ffn.reshape(1, d), mod, mod)


def _ffn_kernel(h_ref, wg_ref, wu_ref, wd_ref, x1_ref, gate_ref, gpost_ref, o_ref, acc_ref):
    j = pl.program_id(1)

    @pl.when(j == 0)
    def _():
        acc_ref[...] = jnp.zeros_like(acc_ref)

    h = h_ref[...]
    a = _dot(h, wg_ref[...])
    act = (a * jax.nn.sigmoid(a) * _dot(h, wu_ref[...])).astype(BF)
    acc_ref[...] += _dot(act, wd_ref[...])

    @pl.when(j == pl.num_programs(1) - 1)
    def _():
        o_ref[...] = x1_ref[...] + _rows(gate_ref) * (_rms(acc_ref[...]) * gpost_ref[...])


def _ffn(h2, wg, wu, wd, x1, mod, g_post_ffn, tm, row_of_tile):
    m, d = x1.shape
    dff = wg.shape[1]
    tf = 512
    if row_of_tile is None:
        gate_spec = pl.BlockSpec((tm, d), lambda i, j: (i, 5))
    else:
        gate_spec = pl.BlockSpec((1, 1, d), lambda i, j: (row_of_tile(i), 0, 5))
    return pl.pallas_call(
        _ffn_kernel,
        out_shape=jax.ShapeDtypeStruct((m, d), F32),
        grid=(m // tm, dff // tf),
        in_specs=[pl.BlockSpec((tm, d), lambda i, j: (i, 0)),
                  pl.BlockSpec((d, tf), lambda i, j: (0, j)),
                  pl.BlockSpec((d, tf), lambda i, j: (0, j)),
                  pl.BlockSpec((tf, d), lambda i, j: (j, 0)),
                  pl.BlockSpec((tm, d), lambda i, j: (i, 0)),
                  gate_spec,
                  pl.BlockSpec((1, d), lambda i, j: (0, 0))],
        out_specs=pl.BlockSpec((tm, d), lambda i, j: (i, 0)),
        scratch_shapes=[pltpu.VMEM((tm, d), F32)],
        compiler_params=_cparams(2),
        name="ffn",
    )(h2, wg, wu, wd, x1, mod, g_post_ffn.reshape(1, d))


def _split_w_in(w_in, d_model):
    q = N_H_NSA * HEAD_DIM
    kv = N_KV * HEAD_DIM
    sb = N_H_SB * HEAD_DIM
    gn = N_H_NSA * 3
    o = 0
    cols = {}
    for name, width in (("q_nsa", q), ("cmp", 2 * kv), ("sel", 2 * kv), ("win", 2 * kv),
                        ("g", gn), ("q_sb", sb), ("kv_sb", 2 * sb), ("m", 2 * d_model)):
        cols[name] = w_in[:, o:o + width].astype(BF)
        o += width
    cols["g"] = jnp.pad(cols["g"], ((0, 0), (0, HEAD_DIM - gn)))
    return cols


def _group(x2d, mod, row_of_tile, tm, pos, w, tail_w):
    g_pre_mix = tail_w["g_pre_mix"]
    h = _prenorm(x2d, g_pre_mix, mod, 1, 0, tm, row_of_tile)
    cos, sin = _rope_tables(pos)
    proj = {}
    proj["q_nsa"], = _project(h, w["q_nsa"], cos, sin, tm, 4, (BF,))
    proj["cmp"], = _project(h, w["cmp"], cos, sin, tm, 0, (), cache_rows=True)
    proj["sel"], proj["sel_bf"] = _project(h, w["sel"], cos, sin, tm, N_KV, (BF,), cache_rows=True)
    proj["win"], proj["win_bf"] = _project(h, w["win"], cos, sin, tm, N_KV, (BF,), cache_rows=True)
    proj["g"], = _project(h, w["g"], cos, sin, tm, 0, (F32,))
    proj["q_sb"], = _project(h, w["q_sb"], cos, sin, tm, 0, (BF,))
    proj["kv_sb"], proj["kv_sb_bf"] = _project(h, w["kv_sb"], cos, sin, tm, 0, (BF,),
                                               cache_rows=True)
    proj["m"], = _project(h, w["m"], cos, sin, tm, 0, (F32,))
    return proj


def _tail(x2d, o_nsa, o_sb, m_all, mod, row_of_tile, tm, tw):
    y = _branch_mix(o_nsa, o_sb, tw["w_br_nsa"], tw["w_br_sb"], m_all, tm)
    x1, h2 = _post_mix(y, tw["w_out"], x2d, mod, tw["g_post_mix"], tw["g_pre_ffn"], tm,
                       row_of_tile)
    return _ffn(h2, tw["w_ffn_gate"], tw["w_ffn_up"], tw["w_ffn_down"], x1, mod,
                tw["g_post_ffn"], tm, row_of_tile)


def kernel(x_prompt, x_sample, c_prompt, c_sample, cache_cmp, cache_sel, cache_sb, cache_win,
           page_table, w_ada, b_ada, g_pre_mix, g_post_mix, g_pre_ffn, g_post_ffn, w_in,
           pe_ck, w_ck1, b_ck1, w_ck2, pe_cv, w_cv1, b_cv1, w_cv2,
           w_br_nsa, w_br_sb, w_out, w_ffn_gate, w_ffn_up, w_ffn_down):
    n_p, t_p, d = x_prompt.shape
    n_s, t_s, _ = x_sample.shape
    assert t_s == 1
    n_pages = page_table.shape[1]
    past = n_pages * PAGE
    w_buf = cache_win.shape[1]
    kvw = 2 * N_KV * HEAD_DIM

    w = _split_w_in(w_in, d)
    tw = dict(g_pre_mix=g_pre_mix, g_post_mix=g_post_mix, g_pre_ffn=g_pre_ffn,
              g_post_ffn=g_post_ffn, w_br_nsa=w_br_nsa.astype(BF), w_br_sb=w_br_sb.astype(BF),
              w_out=w_out.astype(BF), w_ffn_gate=w_ffn_gate.astype(BF),
              w_ffn_up=w_ffn_up.astype(BF), w_ffn_down=w_ffn_down.astype(BF))
    cmp_w = _cmp_weights(pe_ck, w_ck1, b_ck1, w_ck2, pe_cv, w_cv1, b_cv1, w_cv2)

    n_mod = -(-(n_s + n_p) // 16) * 16
    c_all = jnp.concatenate([c_sample, c_prompt, jnp.zeros((n_mod - n_s - n_p, d), F32)])
    mod = _adaln(c_all, w_ada, b_ada)
    mod3 = mod.reshape(n_mod, 1, 6 * d)

    tm_p = 512
    tiles_per_seq = t_p // tm_p
    row_p = lambda i: n_s + i // tiles_per_seq
    xp = x_prompt.reshape(n_p * t_p, d)
    pp = _group(xp, mod3, row_p, tm_p, jnp.arange(t_p), w, tw)
    kvc = _compress_prompt(pp["cmp"], cmp_w, n_p, t_p)
    tq = 128
    o_nsa = _nsa_prompt(pp["q_nsa"], kvc, pp["sel_bf"], pp["win_bf"], pp["g"], n_p, t_p, tq)
    o_sb = _sb_prompt(pp["q_sb"], pp["kv_sb_bf"], n_p, t_p, tq)
    y_prompt = _tail(xp, o_nsa, o_sb, pp["m"], mod3, row_p, tm_p, tw).reshape(n_p, t_p, d)
    new_cmp_p = pp["cmp"].reshape(n_p, t_p, 2, N_KV, HEAD_DIM)
    new_sel_p = pp["sel"].reshape(n_p, t_p, 2, N_KV, HEAD_DIM)
    new_sb_p = pp["kv_sb"].reshape(n_p, t_p, 2, N_H_SB, HEAD_DIM)
    win_rows = pp["win"].reshape(n_p, t_p, 2, N_KV, HEAD_DIM)
    if t_p >= w_buf:
        new_win_p = win_rows[:, t_p - w_buf:]
    else:
        new_win_p = jnp.pad(win_rows, ((0, 0), (w_buf - t_p, 0), (0, 0), (0, 0), (0, 0)))

    xs = x_sample.reshape(n_s, d)
    pos_s = jnp.full((n_s,), past, jnp.int32)
    ps = _group(xs, mod, None, n_s, pos_s, w, tw)
    o_c, idx = _cmp_sample(ps["q_nsa"], cache_cmp, page_table, cmp_w, past)
    idx = idx[:, :, 0, :min(N_SELECT, -(-(past + 1) // SEL_BLOCK))]
    o_s = _sel_sample(ps["q_nsa"], idx, ps["sel"], cache_sel, page_table)
    o_nsa_s, new_win_s = _win_sample(ps["q_nsa"], cache_win, ps["win"], o_c, o_s, ps["g"])
    o_sb_s = _sb_sample(ps["q_sb"], cache_sb, page_table)
    y_sample = _tail(xs, o_nsa_s, o_sb_s, ps["m"], mod, None, n_s, tw).reshape(n_s, 1, d)

    return (y_prompt, y_sample, new_cmp_p, new_sel_p, new_sb_p, new_win_p,
            ps["cmp"].reshape(n_s, 1, 2, N_KV, HEAD_DIM),
            ps["sel"].reshape(n_s, 1, 2, N_KV, HEAD_DIM),
            ps["kv_sb"].reshape(n_s, 1, 2, N_H_SB, HEAD_DIM),
            new_win_s.reshape(n_s, w_buf, 2, N_KV, HEAD_DIM))
```

```python
import functools

import numpy as np
import jax
import jax.numpy as jnp
from jax import lax
from jax.experimental import pallas as pl
from jax.experimental.pallas import tpu as pltpu

BF = jnp.bfloat16
F32 = jnp.float32

HEAD_DIM = 128
N_KV = 2
GROUP_R = 4
N_H_NSA = N_KV * GROUP_R
N_H_SB = 8
L_CMP = 32
STRIDE = 16
SEL_BLOCK = 64
N_SELECT = 16
WINDOW = 512
PAGE = 128
CMP_HIDDEN = 256
ROPE_THETA = 10000.0
EPS = 1e-6
SCALE = HEAD_DIM ** -0.5
NEG = -1e30
VMEM_LIMIT = 56 * 1024 * 1024


def _cparams(n_axes):
    return pltpu.CompilerParams(dimension_semantics=("arbitrary",) * n_axes,
                                vmem_limit_bytes=VMEM_LIMIT)


def _dot(a, b):
    return jnp.dot(a, b, preferred_element_type=F32)


def _dot_nt(a, b):
    return lax.dot_general(a, b, (((1,), (1,)), ((), ())), preferred_element_type=F32)


def _split3(x):
    hi = x.astype(BF)
    r = x - hi.astype(F32)
    mid = r.astype(BF)
    lo = (r - mid.astype(F32)).astype(BF)
    return hi, mid, lo


def _softmax_rows(s, valid):
    s = jnp.where(valid, s, NEG)
    m = jnp.max(s, axis=-1, keepdims=True)
    m = jnp.where(m > 0.5 * NEG, m, 0.0)
    p = jnp.where(valid, jnp.exp(s - m), 0.0)
    return p / jnp.maximum(jnp.sum(p, axis=-1, keepdims=True), 1e-30)


def _topk_select(val, k, lane):
    big = jnp.int32(1 << 30)
    sel = jnp.zeros(val.shape, F32)
    picks = []
    for _ in range(k):
        m = jnp.max(val, axis=-1, keepdims=True)
        idx = jnp.min(jnp.where(val == m, lane, big), axis=-1, keepdims=True)
        hit = lane == idx
        sel = jnp.where(hit, 1.0, sel)
        val = jnp.where(hit, -2.0, val)
        picks.append(idx)
    return sel, picks


def _block_scores(imp, mmat_ref):
    hi, mid, lo = _split3(imp)
    m = mmat_ref[...]
    return _dot(hi, m) + _dot(mid, m) + _dot(lo, m)


def _rope_tables(pos):
    half = HEAD_DIM // 2
    inv = ROPE_THETA ** (-2.0 * jnp.arange(half, dtype=F32) / HEAD_DIM)
    ang = pos.astype(F32)[:, None] * inv[None, :]
    c, s = jnp.cos(ang), jnp.sin(ang)
    return jnp.concatenate([c, c], axis=1), jnp.concatenate([-s, s], axis=1)


def _rope(x, cos, sin_signed):
    return x * cos + pltpu.roll(x, HEAD_DIM // 2, 1) * sin_signed


def _score_matrix(n_c_pad, n_blk_pad):
    j = np.arange(n_c_pad)[:, None]
    b = np.arange(n_blk_pad)[None, :]
    spb = SEL_BLOCK // STRIDE
    m = ((j == spb * b - 1).astype(np.float32) + (j == spb * b + spb - 1)
         + 2.0 * ((j >= spb * b) & (j <= spb * b + spb - 2)))
    return jnp.asarray(m, BF)


def _ada_kernel(c_ref, w_ref, b_ref, o_ref):
    o_ref[...] = _dot(c_ref[...], w_ref[...].astype(BF)) + b_ref[...]


def _adaln(c_pad, w_ada, b_ada):
    m, d = c_pad.shape
    n = w_ada.shape[1]
    tn = 1536
    return pl.pallas_call(
        _ada_kernel,
        out_shape=jax.ShapeDtypeStruct((m, n), F32),
        grid=(n // tn,),
        in_specs=[pl.BlockSpec((m, d), lambda j: (0, 0)),
                  pl.BlockSpec((d, tn), lambda j: (0, j)),
                  pl.BlockSpec((1, tn), lambda j: (0, j))],
        out_specs=pl.BlockSpec((m, tn), lambda j: (0, j)),
        compiler_params=_cparams(1),
        name="adaln",
    )(c_pad.astype(BF), w_ada, b_ada.reshape(1, n))


def _rows(ref):
    v = ref[...]
    return v.reshape(v.shape[-2], v.shape[-1])


def _prenorm_kernel(x_ref, g_ref, sc_ref, sh_ref, o_ref):
    x = x_ref[...]
    y = x * lax.rsqrt(jnp.mean(x * x, axis=-1, keepdims=True) + EPS)
    o_ref[...] = (y * g_ref[...] * (1.0 + _rows(sc_ref)) + _rows(sh_ref)).astype(o_ref.dtype)


def _mod_spec(tm, d, k, row_of_tile):
    if row_of_tile is None:
        return pl.BlockSpec((tm, d), lambda i: (i, k))
    return pl.BlockSpec((1, 1, d), lambda i: (row_of_tile(i), 0, k))


def _prenorm(x2d, g, mod, k_scale, k_shift, tm, row_of_tile):
    m, d = x2d.shape
    return pl.pallas_call(
        _prenorm_kernel,
        out_shape=jax.ShapeDtypeStruct((m, d), BF),
        grid=(m // tm,),
        in_specs=[pl.BlockSpec((tm, d), lambda i: (i, 0)),
                  pl.BlockSpec((1, d), lambda i: (0, 0)),
                  _mod_spec(tm, d, k_scale, row_of_tile),
                  _mod_spec(tm, d, k_shift, row_of_tile)],
        out_specs=pl.BlockSpec((tm, d), lambda i: (i, 0)),
        compiler_params=_cparams(1),
        name="prenorm",
    )(x2d, g.reshape(1, d), mod, mod)


def _proj_kernel(h_ref, w_ref, cos_ref, sin_ref, *o_refs, rope_heads, tn, cache_rows):
    y = _dot(h_ref[...], w_ref[...])
    n_head = tn // HEAD_DIM
    parts = [y[:, hd * HEAD_DIM:(hd + 1) * HEAD_DIM] for hd in range(n_head)]
    if rope_heads:
        cos, sin = cos_ref[...], sin_ref[...]
        parts = [_rope(p, cos, sin) if hd < rope_heads else p for hd, p in enumerate(parts)]
        y = jnp.concatenate(parts, axis=1)
    if cache_rows:
        tm = y.shape[0]
        for hd, p in enumerate(parts):
            o_refs[0][pl.ds(hd, tm, stride=n_head), :] = p
        o_refs = o_refs[1:]
    for o_ref in o_refs:
        o_ref[...] = y.astype(o_ref.dtype)


def _project(h, w, cos, sin, tm, rope_heads, out_dtypes, cache_rows=False):
    m, d = h.shape
    n = w.shape[1]
    tn = n if cache_rows else min(n, 512)
    t_tiles = cos.shape[0] // tm
    n_head = n // HEAD_DIM
    out_shape = [jax.ShapeDtypeStruct((m, n), dt) for dt in out_dtypes]
    out_specs = [pl.BlockSpec((tm, tn), lambda j, i: (i, j)) for _ in out_dtypes]
    if cache_rows:
        out_shape.insert(0, jax.ShapeDtypeStruct((m * n_head, HEAD_DIM), F32))
        out_specs.insert(0, pl.BlockSpec((tm * n_head, HEAD_DIM), lambda j, i: (i, 0)))
    outs = pl.pallas_call(
        functools.partial(_proj_kernel, rope_heads=rope_heads, tn=tn, cache_rows=cache_rows),
        out_shape=out_shape,
        grid=(n // tn, m // tm),
        in_specs=[pl.BlockSpec((tm, d), lambda j, i: (i, 0)),
                  pl.BlockSpec((d, tn), lambda j, i: (0, j)),
                  pl.BlockSpec((tm, HEAD_DIM), lambda j, i: (i % t_tiles, 0)),
                  pl.BlockSpec((tm, HEAD_DIM), lambda j, i: (i % t_tiles, 0))],
        out_specs=out_specs,
        compiler_params=_cparams(2),
        name="project",
    )(h, w, cos, sin)
    return outs


def _gelu(x):
    return 0.5 * x * (1.0 + jnp.tanh(0.7978845608028654 * (x + 0.044715 * x * x * x)))


def _cmp_bias(pe_ref, w1_ref, b1_ref):
    pb = _dot(pe_ref[0], w1_ref[0])
    return pb[0:1, :CMP_HIDDEN] + pb[1:2, CMP_HIDDEN:] + b1_ref[0]


def _cmp_finish(first, second, bias, w2, cos, sin, is_key, nseg):
    nxt = pltpu.roll(second, nseg - 1, 0)
    hid = _gelu(first + nxt + bias).astype(BF)
    out = _dot(hid, w2)
    out = jnp.where(is_key, _rope(out, cos, sin), out)
    row = lax.broadcasted_iota(jnp.int32, out.shape, 0)
    return jnp.where(row < nseg - 1, out, 0.0)


def _cmp_prompt_kernel(x_ref, w1_ref, pe_ref, b1_ref, w2_ref, cos_ref, sin_ref, o_ref, *, nseg):
    kv = pl.program_id(1)
    n_col = 2 * N_KV
    bias = _cmp_bias(pe_ref, w1_ref, b1_ref)
    for g in range(N_KV):
        segs = []
        for c in (g, N_KV + g):
            segs.append(jnp.concatenate(
                [x_ref[0, pl.ds(r * n_col + c, nseg, stride=STRIDE * n_col), :]
                 for r in range(STRIDE)], axis=1))
        seg = jnp.where(kv == 0, segs[0], segs[1]).astype(BF)
        fs = _dot(seg, w1_ref[0])
        out = _cmp_finish(fs[:, :CMP_HIDDEN], fs[:, CMP_HIDDEN:], bias, w2_ref[0],
                          cos_ref[...], sin_ref[...], kv == 0, nseg)
        o_ref[0, g] = out.astype(o_ref.dtype)


def _cmp_weights(pe_ck, w_ck1, b_ck1, w_ck2, pe_cv, w_cv1, b_cv1, w_cv2):
    half = STRIDE * HEAD_DIM

    def w1cat(w):
        return jnp.concatenate([w[:half], w[half:]], axis=1)

    w1 = jnp.stack([w1cat(w_ck1), w1cat(w_cv1)]).astype(BF)
    pe = jnp.stack([pe_ck.reshape(2, half), pe_cv.reshape(2, half)])
    pe = jnp.pad(pe, ((0, 0), (0, 6), (0, 0))).astype(BF)
    b1 = jnp.stack([b_ck1, b_cv1]).reshape(2, 1, CMP_HIDDEN)
    w2 = jnp.stack([w_ck2, w_cv2]).astype(BF)
    return w1, pe, b1, w2


def _compress_prompt(kv_cmp, cmp_w, n, t):
    w1, pe, b1, w2 = cmp_w
    nseg = t // STRIDE
    n_col = 2 * N_KV
    cos, sin = _rope_tables(jnp.arange(nseg) * STRIDE)
    x3 = kv_cmp.reshape(n, t * n_col, HEAD_DIM)
    return pl.pallas_call(
        functools.partial(_cmp_prompt_kernel, nseg=nseg),
        out_shape=jax.ShapeDtypeStruct((n, n_col, nseg, HEAD_DIM), BF),
        grid=(n, 2),
        in_specs=[pl.BlockSpec((1, t * n_col, HEAD_DIM), lambda b, kv: (b, 0, 0)),
                  pl.BlockSpec((1,) + w1.shape[1:], lambda b, kv: (kv, 0, 0)),
                  pl.BlockSpec((1,) + pe.shape[1:], lambda b, kv: (kv, 0, 0)),
                  pl.BlockSpec((1,) + b1.shape[1:], lambda b, kv: (kv, 0, 0)),
                  pl.BlockSpec((1,) + w2.shape[1:], lambda b, kv: (kv, 0, 0)),
                  pl.BlockSpec((nseg, HEAD_DIM), lambda b, kv: (0, 0)),
                  pl.BlockSpec((nseg, HEAD_DIM), lambda b, kv: (0, 0))],
        out_specs=pl.BlockSpec((1, N_KV, nseg, HEAD_DIM), lambda b, kv: (b, kv, 0, 0)),
        compiler_params=_cparams(2),
        name="compress_prompt",
    )(x3, w1, pe, b1, w2, cos, sin)


def _flash_step(qs, k, v, valid, m, l, acc):
    tq, tk = valid.shape
    s = (_dot_nt(qs, k) * SCALE).reshape(GROUP_R, tq, tk)
    s = jnp.where(valid[None], s, NEG).reshape(GROUP_R * tq, tk)
    m_new = jnp.maximum(m, jnp.max(s, axis=-1, keepdims=True))
    alpha = jnp.exp(m - m_new)
    p = jnp.exp(s - m_new)
    l = alpha * l + jnp.sum(p, axis=-1, keepdims=True)
    acc = alpha * acc + _dot(p.astype(BF), v)
    return m_new, l, acc


def _rank_select_t(val_t, k, n_blk):
    blk = lax.broadcasted_iota(jnp.int32, val_t.shape, 0)
    cnt = jnp.zeros(val_t.shape, F32)
    for other in range(n_blk):
        vo = val_t[other:other + 1, :]
        beats = (vo > val_t) | ((vo == val_t) & (blk > other))
        cnt = cnt + jnp.where(beats, 1.0, 0.0)
    return jnp.where((cnt < k) & (val_t >= 0.0), 1.0, 0.0)


def _nsa_prompt_kernel(q_ref, kc_ref, vc_ref, ks_ref, vs_ref, kw_ref, vw_ref, g_ref, mm_ref,
                       o_ref, *, tq, n_c, n_blk):
    g_idx = pl.program_id(1)
    qi = pl.program_id(2)
    q = q_ref[...]
    qs = jnp.concatenate([q[:, h * HEAD_DIM:(h + 1) * HEAD_DIM] for h in range(GROUP_R)], axis=0)
    rows = GROUP_R * tq
    qpos = qi * tq + lax.broadcasted_iota(jnp.int32, (tq, 1), 0)

    n_cp = kc_ref.shape[2]
    s = (_dot_nt(qs, kc_ref[0, 0]) * SCALE).reshape(GROUP_R, tq, n_cp)
    jc = lax.broadcasted_iota(jnp.int32, (tq, n_cp), 1)
    valid_c = (jc * STRIDE + (L_CMP - 1) <= qpos) & (jc < n_c)
    p_c = _softmax_rows(s, valid_c[None])
    o_c = _dot(p_c.reshape(rows, n_cp).astype(BF), vc_ref[0, 0])
    imp = jnp.sum(p_c, axis=0)

    mm_t = mm_ref[...]
    n_lane = mm_t.shape[0]
    score_t = sum(_dot_nt(mm_t, part) for part in _split3(imp))
    blk = lax.broadcasted_iota(jnp.int32, (n_lane, tq), 0)
    cur = (qi * tq + lax.broadcasted_iota(jnp.int32, (1, tq), 1)) // SEL_BLOCK
    forced = (blk == 0) | (blk == cur) | (blk == cur - 1)
    val_t = jnp.where(blk > cur, -1.0, jnp.where(forced, 3e38, score_t))
    sel_t = _rank_select_t(val_t[:n_blk], min(N_SELECT, n_blk), n_blk)
    if n_blk < n_lane:
        sel_t = jnp.concatenate([sel_t, jnp.zeros((n_lane - n_blk, tq), F32)], axis=0)
    sel = sel_t.T.astype(BF)

    lane_k = lax.broadcasted_iota(jnp.int32, (tq, tq), 1)
    eb = lax.broadcasted_iota(jnp.int32, (n_lane, tq), 0)
    ej = lax.broadcasted_iota(jnp.int32, (n_lane, tq), 1) // SEL_BLOCK
    init = (jnp.full((rows, 1), 0.5 * NEG, F32), jnp.zeros((rows, 1), F32),
            jnp.zeros((rows, HEAD_DIM), F32))

    def sel_body(kt, carry):
        k0 = pl.multiple_of(kt * tq, tq)
        expand = jnp.where(eb == ej + kt * (tq // SEL_BLOCK), 1.0, 0.0).astype(BF)
        chosen = _dot(sel, expand) > 0.5
        valid = chosen & (k0 + lane_k <= qpos)
        return _flash_step(qs, ks_ref[pl.ds(k0, tq), :], vs_ref[pl.ds(k0, tq), :], valid, *carry)

    _, l_s, acc_s = lax.fori_loop(0, qi + 1, sel_body, init)
    o_s = acc_s / jnp.maximum(l_s, 1e-30)

    def win_body(kt, carry):
        k0 = pl.multiple_of(kt * tq, tq)
        dist = qpos - (k0 + lane_k)
        valid = (dist >= 0) & (dist < WINDOW)
        return _flash_step(qs, kw_ref[pl.ds(k0, tq), :], vw_ref[pl.ds(k0, tq), :], valid, *carry)

    _, l_w, acc_w = lax.fori_loop(jnp.maximum(qi - WINDOW // tq, 0), qi + 1, win_body, init)
    o_w = acc_w / jnp.maximum(l_w, 1e-30)

    gate = jax.nn.sigmoid(g_ref[...])
    lane_g = lax.broadcasted_iota(jnp.int32, gate.shape, 1)
    outs = []
    for h in range(GROUP_R):
        base = (g_idx * GROUP_R + h) * 3
        g3 = [jnp.sum(jnp.where(lane_g == base + b, gate, 0.0), axis=-1, keepdims=True)
              for b in range(3)]
        sl = slice(h * tq, (h + 1) * tq)
        outs.append(g3[0] * o_c[sl] + g3[1] * o_s[sl] + g3[2] * o_w[sl])
    o_ref[...] = jnp.concatenate(outs, axis=1).astype(o_ref.dtype)


def _nsa_prompt(q_nsa, kvc, sel_bf, win_bf, g_nsa, n, t, tq):
    nseg = kvc.shape[2]
    n_c = nseg - 1
    n_blk = -(-t // SEL_BLOCK)
    qt = t // tq
    gw = GROUP_R * HEAD_DIM
    mmat = _score_matrix(nseg, 128).T
    kv_spec = lambda col: pl.BlockSpec((t, HEAD_DIM), lambda b, g, i: (b, col + g))
    return pl.pallas_call(
        functools.partial(_nsa_prompt_kernel, tq=tq, n_c=n_c, n_blk=n_blk),
        out_shape=jax.ShapeDtypeStruct((n * t, N_H_NSA * HEAD_DIM), BF),
        grid=(n, N_KV, qt),
        in_specs=[pl.BlockSpec((tq, gw), lambda b, g, i: (b * qt + i, g)),
                  pl.BlockSpec((1, 1, nseg, HEAD_DIM), lambda b, g, i: (b, g, 0, 0)),
                  pl.BlockSpec((1, 1, nseg, HEAD_DIM), lambda b, g, i: (b, N_KV + g, 0, 0)),
                  kv_spec(0), kv_spec(N_KV), kv_spec(0), kv_spec(N_KV),
                  pl.BlockSpec((tq, HEAD_DIM), lambda b, g, i: (b * qt + i, 0)),
                  pl.BlockSpec(mmat.shape, lambda b, g, i: (0, 0))],
        out_specs=pl.BlockSpec((tq, gw), lambda b, g, i: (b * qt + i, g)),
        compiler_params=_cparams(3),
        name="nsa_prompt",
    )(q_nsa, kvc, kvc, sel_bf, sel_bf, win_bf, win_bf, g_nsa, mmat)


def _softplus(z):
    return jnp.maximum(z, 0.0) + jnp.log1p(jnp.exp(-jnp.abs(z)))


def _strict_upper(n):
    return jnp.where(lax.broadcasted_iota(jnp.int32, (n, n), 0)
                     > lax.broadcasted_iota(jnp.int32, (n, n), 1), 1.0, 0.0).astype(BF)


def _suffix_sum(lm, u):
    hi = lm.astype(BF)
    lo = (lm - hi.astype(F32)).astype(BF)
    return _dot(hi, u) + _dot(lo, u)


SB_DEAD = -110.0


def _sb_tile(q, k, v, u, causal, run, acc):
    z = _dot_nt(q, k) * SCALE
    sp = _softplus(z)
    lm = -sp if causal is None else jnp.where(causal, -sp, 0.0)
    a = jnp.exp(z - sp + _suffix_sum(lm, u) + run)
    if causal is not None:
        a = jnp.where(causal, a, 0.0)
    acc = acc + _dot(a.astype(BF), v)
    return run + jnp.sum(lm, axis=-1, keepdims=True), acc


def _sb_prompt_kernel(q_ref, k_ref, v_ref, o_ref, *, tq, nh):
    qi = pl.program_id(2)
    u = _strict_upper(tq)
    qpos = qi * tq + lax.broadcasted_iota(jnp.int32, (tq, 1), 0)
    lane = lax.broadcasted_iota(jnp.int32, (tq, tq), 1)

    def cond(st):
        return (st[0] <= qi) & (st[1] > 0)

    def body(st):
        step = st[0]
        k0 = pl.multiple_of((qi - step) * tq, tq)
        causal = k0 + lane < qpos
        runs, accs, top = [], [], None
        for h in range(nh):
            cols = slice(h * HEAD_DIM, (h + 1) * HEAD_DIM)
            run, acc = _sb_tile(q_ref[:, cols], k_ref[pl.ds(k0, tq), cols],
                                v_ref[pl.ds(k0, tq), cols], u, causal, st[2 + h], st[2 + nh + h])
            runs.append(run)
            accs.append(acc)
            top = jnp.max(run) if top is None else jnp.maximum(top, jnp.max(run))
        return (step + 1, (top > SB_DEAD).astype(jnp.int32), *runs, *accs)

    init = ((jnp.int32(0), jnp.int32(1)) + tuple(jnp.zeros((tq, 1), F32) for _ in range(nh))
            + tuple(jnp.zeros((tq, HEAD_DIM), F32) for _ in range(nh)))
    st = lax.while_loop(cond, body, init)
    o_ref[...] = jnp.concatenate(st[2 + nh:], axis=1).astype(o_ref.dtype)


def _sb_prompt(q_sb, kv_sb_bf, n, t, tq):
    qt = t // tq
    nh = 4
    ng = N_H_SB // nh
    return pl.pallas_call(
        functools.partial(_sb_prompt_kernel, tq=tq, nh=nh),
        out_shape=jax.ShapeDtypeStruct((n * t, N_H_SB * HEAD_DIM), BF),
        grid=(n, ng, qt),
        in_specs=[pl.BlockSpec((tq, nh * HEAD_DIM), lambda b, h, i: (b * qt + i, h)),
                  pl.BlockSpec((t, nh * HEAD_DIM), lambda b, h, i: (b, h)),
                  pl.BlockSpec((t, nh * HEAD_DIM), lambda b, h, i: (b, ng + h))],
        out_specs=pl.BlockSpec((tq, nh * HEAD_DIM), lambda b, h, i: (b * qt + i, h)),
        compiler_params=_cparams(3),
        name="sb_prompt",
    )(q_sb, kv_sb_bf, kv_sb_bf)


def _sb_sample_kernel(pt_ref, q_ref, cache_ref, o_ref, buf, sem, *, n_pg, n_pages):
    b = pl.program_id(0)
    n_chunk = n_pages // n_pg
    rows_per_tok = 2 * N_H_SB

    def page_copy(chunk, slot, pg):
        page = pt_ref[b * n_pages + (n_chunk - 1 - chunk) * n_pg + pg]
        return pltpu.make_async_copy(cache_ref.at[page], buf.at[slot, pg], sem.at[slot, pg])

    def start(chunk, slot):
        for pg in range(n_pg):
            page_copy(chunk, slot, pg).start()

    def wait(chunk, slot):
        for pg in range(n_pg):
            page_copy(chunk, slot, pg).wait()

    start(0, 0)
    q = q_ref[0]
    u = _strict_upper(PAGE)
    row = lax.broadcasted_iota(jnp.int32, (N_H_SB, PAGE), 0)

    def cond(st):
        return (st[0] < n_chunk) & (st[1] > 0)

    def body(st):
        c, _, run, acc = st
        slot = c % 2
        wait(c, slot)

        @pl.when(c + 1 < n_chunk)
        def _():
            start(c + 1, 1 - slot)

        for pg in reversed(range(n_pg)):
            page = buf.at[slot, pg]
            z = jnp.zeros((N_H_SB, PAGE), F32)
            for h in range(N_H_SB):
                kh = page[pl.ds(h, PAGE, stride=rows_per_tok), :].astype(BF)
                z = jnp.where(row == h, _dot_nt(q, kh), z)
            z = z * SCALE
            sp = _softplus(z)
            lm = -sp
            a = jnp.exp(z - sp + _suffix_sum(lm, u) + run).astype(BF)
            for h in range(N_H_SB):
                vh = page[pl.ds(N_H_SB + h, PAGE, stride=rows_per_tok), :].astype(BF)
                acc = acc + jnp.where(row == h, _dot(a, vh), 0.0)
            run = run + jnp.sum(lm, axis=-1, keepdims=True)
        return c + 1, (jnp.max(run) > SB_DEAD).astype(jnp.int32), run, acc

    c, _, _, acc = lax.while_loop(
        cond, body, (jnp.int32(0), jnp.int32(1), jnp.zeros((N_H_SB, 1), F32),
                     jnp.zeros((N_H_SB, HEAD_DIM), F32)))

    @pl.when(c < n_chunk)
    def _():
        wait(c, c % 2)

    o_ref[0] = acc.astype(o_ref.dtype)


def _sb_sample(q_sb, cache_sb, page_table):
    bsz, n_pages = page_table.shape
    n_phys = cache_sb.shape[0]
    n_pg = 2
    hw = N_H_SB * HEAD_DIM
    page_rows = PAGE * 2 * N_H_SB
    cache = cache_sb.reshape(n_phys, page_rows, HEAD_DIM)
    out = pl.pallas_call(
        functools.partial(_sb_sample_kernel, n_pg=n_pg, n_pages=n_pages),
        out_shape=jax.ShapeDtypeStruct((bsz, N_H_SB, HEAD_DIM), BF),
        grid_spec=pltpu.PrefetchScalarGridSpec(
            num_scalar_prefetch=1,
            grid=(bsz,),
            in_specs=[pl.BlockSpec((1, N_H_SB, HEAD_DIM), lambda b, pt: (b, 0, 0)),
                      pl.BlockSpec(memory_space=pl.ANY)],
            out_specs=pl.BlockSpec((1, N_H_SB, HEAD_DIM), lambda b, pt: (b, 0, 0)),
            scratch_shapes=[pltpu.VMEM((2, n_pg, page_rows, HEAD_DIM), F32),
                            pltpu.SemaphoreType.DMA((2, n_pg))]),
        compiler_params=_cparams(1),
        name="sb_sample",
    )(page_table.reshape(-1), q_sb.reshape(bsz, N_H_SB, HEAD_DIM), cache)
    return out.reshape(bsz, hw)


def _cmp_sample_kernel(pt_ref, q_ref, w1_ref, pe_ref, b1_ref, w2_ref, cos_ref, sin_ref, mm_ref,
                       *refs, n_pg, nseg, n_blk, q_pos):
    n_col = 2 * N_KV
    page_refs = refs[:n_pg]
    oc_ref, idx_ref = refs[n_pg:n_pg + 2]
    fs_ref = refs[n_pg + 2]
    j = pl.program_id(1)
    spp = PAGE // STRIDE
    rows = n_pg * spp
    row0 = pl.multiple_of(j * rows, rows)
    for kv in range(2):
        segs = []
        for c in range(kv * N_KV, (kv + 1) * N_KV):
            for pg in range(n_pg):
                segs.append(jnp.concatenate(
                    [page_refs[pg][0, pl.ds(r * n_col + c, spp, stride=STRIDE * n_col), :]
                     for r in range(STRIDE)], axis=1))
        segs = jnp.concatenate(segs, axis=0).astype(BF)
        prod = _dot(segs, w1_ref[kv])
        for g in range(N_KV):
            fs_ref[kv * N_KV + g, pl.ds(row0, rows), :] = prod[g * rows:(g + 1) * rows]

    @pl.when(j == pl.num_programs(1) - 1)
    def _():
        n_c = nseg - 1
        cos, sin = cos_ref[...], sin_ref[...]
        comp = []
        for c in range(2 * N_KV):
            kv = c // N_KV
            pb = _dot(pe_ref[kv], w1_ref[kv])
            bias = pb[0:1, :CMP_HIDDEN] + pb[1:2, CMP_HIDDEN:] + b1_ref[kv]
            fs = fs_ref[c]
            comp.append(_cmp_finish(fs[:, :CMP_HIDDEN], fs[:, CMP_HIDDEN:], bias, w2_ref[kv],
                                    cos, sin, kv == 0, nseg).astype(BF))
        jc = lax.broadcasted_iota(jnp.int32, (8, nseg), 1)
        valid = (jc * STRIDE + (L_CMP - 1) <= q_pos) & (jc < n_c)
        head = lax.broadcasted_iota(jnp.int32, (8, nseg), 0)
        n_lane = mm_ref.shape[1]
        blk = lax.broadcasted_iota(jnp.int32, (8, n_lane), 1)
        cur = q_pos // SEL_BLOCK
        forced = (blk == 0) | (blk == cur) | (blk == cur - 1)
        lane_o = lax.broadcasted_iota(jnp.int32, (8, HEAD_DIM), 1)
        for g in range(N_KV):
            qg = q_ref[0, g]
            s = _dot_nt(qg, comp[g]) * SCALE
            p = _softmax_rows(s, valid)
            oc_ref[0, g] = _dot(p.astype(BF), comp[N_KV + g])
            imp = jnp.sum(jnp.where(head < GROUP_R, p, 0.0), axis=0, keepdims=True)
            score = _block_scores(jnp.broadcast_to(imp, (8, nseg)), mm_ref)
            val = jnp.where((blk > cur) | (blk >= n_blk), -1.0, jnp.where(forced, 3e38, score))
            _, picks = _topk_select(val, min(N_SELECT, n_blk), blk)
            row = jnp.zeros((8, HEAD_DIM), jnp.int32)
            for kk, pick in enumerate(picks):
                row = jnp.where(lane_o == kk, pick, row)
            idx_ref[0, g] = row


def _cmp_sample(q_nsa, cache_cmp, page_table, cmp_w, q_pos):
    w1, pe, b1, w2 = cmp_w
    bsz, n_pages = page_table.shape
    n_phys = cache_cmp.shape[0]
    n_pg = min(32, n_pages)
    n_chunk = n_pages // n_pg
    nseg = n_pages * PAGE // STRIDE
    n_blk = -(-(n_pages * PAGE + 1) // SEL_BLOCK)
    n_lane = -(-n_blk // 128) * 128
    mmat = _score_matrix(nseg, n_lane)
    cos, sin = _rope_tables(jnp.arange(nseg) * STRIDE)
    page_rows = PAGE * 2 * N_KV
    cache = cache_cmp.reshape(n_phys, page_rows, HEAD_DIM)
    q4 = q_nsa.reshape(bsz, N_KV, GROUP_R, HEAD_DIM)
    q4 = jnp.pad(q4, ((0, 0), (0, 0), (0, 8 - GROUP_R), (0, 0)))

    def page_spec(pg):
        return pl.BlockSpec((1, page_rows, HEAD_DIM),
                            lambda b, j, pt: (pt[b * n_pages + j * n_pg + pg], 0, 0))

    page_specs = [page_spec(pg) for pg in range(n_pg)]
    full = lambda a: pl.BlockSpec(a.shape, lambda b, j, pt: (0,) * a.ndim)
    return pl.pallas_call(
        functools.partial(_cmp_sample_kernel, n_pg=n_pg, nseg=nseg, n_blk=n_blk, q_pos=q_pos),
        out_shape=[jax.ShapeDtypeStruct((bsz, N_KV, 8, HEAD_DIM), F32),
                   jax.ShapeDtypeStruct((bsz, N_KV, 8, HEAD_DIM), jnp.int32)],
        grid_spec=pltpu.PrefetchScalarGridSpec(
            num_scalar_prefetch=1,
            grid=(bsz, n_chunk),
            in_specs=[pl.BlockSpec((1, N_KV, 8, HEAD_DIM), lambda b, j, pt: (b, 0, 0, 0)),
                      full(w1), full(pe), full(b1), full(w2), full(cos), full(sin), full(mmat)]
            + page_specs,
            out_specs=[pl.BlockSpec((1, N_KV, 8, HEAD_DIM), lambda b, j, pt: (b, 0, 0, 0)),
                       pl.BlockSpec((1, N_KV, 8, HEAD_DIM), lambda b, j, pt: (b, 0, 0, 0))],
            scratch_shapes=[pltpu.VMEM((2 * N_KV, nseg, 2 * CMP_HIDDEN), F32)]),
        compiler_params=_cparams(2),
        name="cmp_sample",
    )(page_table.reshape(-1), q4, w1, pe, b1, w2, cos, sin, mmat,
      *([cache] * len(page_specs)))


def _sel_sample_kernel(idx_ref, pt_ref, q_ref, new_ref, *refs, n_sel, last_blk):
    o_ref = refs[N_KV * n_sel]
    b = pl.program_id(0)
    n_col = 2 * N_KV
    n_key = n_sel * SEL_BLOCK
    slot = lax.broadcasted_iota(jnp.int32, (8, n_key), 1) // SEL_BLOCK
    new = new_ref[0]
    for g in range(N_KV):
        blks = refs[g * n_sel:(g + 1) * n_sel]
        q = q_ref[0, g]
        keys = jnp.concatenate(
            [r[0, pl.ds(g, SEL_BLOCK, stride=n_col), :] for r in blks], axis=0).astype(BF)
        vals = jnp.concatenate(
            [r[0, pl.ds(N_KV + g, SEL_BLOCK, stride=n_col), :] for r in blks], axis=0).astype(BF)
        s = _dot_nt(q, keys) * SCALE
        valid = jnp.zeros((8, n_key), jnp.bool_)
        has_new = jnp.zeros((8, 1), jnp.bool_)
        for kk in range(n_sel):
            bi = idx_ref[(b * N_KV + g) * n_sel + kk]
            valid = valid | ((slot == kk) & (bi < last_blk))
            has_new = has_new | (bi == last_blk)
        k_new = new[g:g + 1].astype(BF).astype(F32)
        v_new = new[N_KV + g:N_KV + g + 1].astype(BF).astype(F32)
        s_new = jnp.sum(q.astype(F32) * k_new, axis=-1, keepdims=True) * SCALE
        s = jnp.where(valid, s, NEG)
        s_new = jnp.where(has_new, s_new, NEG)
        m = jnp.maximum(jnp.max(s, axis=-1, keepdims=True), s_new)
        m = jnp.where(m > 0.5 * NEG, m, 0.0)
        p = jnp.where(valid, jnp.exp(s - m), 0.0)
        p_new = jnp.where(has_new, jnp.exp(s_new - m), 0.0)
        den = jnp.maximum(jnp.sum(p, axis=-1, keepdims=True) + p_new, 1e-30)
        p = p / den
        p_new = p_new / den
        o_ref[0, g] = _dot(p.astype(BF), vals) + p_new.astype(BF).astype(F32) * v_new


def _sel_sample(q_nsa, idx, new_sel, cache_sel, page_table):
    bsz, n_pages = page_table.shape
    n_phys = cache_sel.shape[0]
    n_sel = idx.shape[2]
    halves = PAGE // SEL_BLOCK
    last_blk = n_pages * halves
    blk_rows = SEL_BLOCK * 2 * N_KV
    cache = cache_sel.reshape(n_phys * halves, blk_rows, HEAD_DIM)
    q4 = q_nsa.reshape(bsz, N_KV, GROUP_R, HEAD_DIM)
    q4 = jnp.pad(q4, ((0, 0), (0, 0), (0, 8 - GROUP_R), (0, 0)))
    new4 = new_sel.reshape(bsz, 2 * N_KV, HEAD_DIM)

    def blk_spec(g, kk):
        def imap(b, idx_r, pt_r):
            bi = jnp.minimum(idx_r[(b * N_KV + g) * n_sel + kk], last_blk - 1)
            return (pt_r[b * n_pages + bi // halves] * halves + bi % halves, 0, 0)
        return pl.BlockSpec((1, blk_rows, HEAD_DIM), imap)

    return pl.pallas_call(
        functools.partial(_sel_sample_kernel, n_sel=n_sel, last_blk=last_blk),
        out_shape=jax.ShapeDtypeStruct((bsz, N_KV, 8, HEAD_DIM), F32),
        grid_spec=pltpu.PrefetchScalarGridSpec(
            num_scalar_prefetch=2,
            grid=(bsz,),
            in_specs=[pl.BlockSpec((1, N_KV, 8, HEAD_DIM), lambda b, i_r, p_r: (b, 0, 0, 0)),
                      pl.BlockSpec((1, 2 * N_KV, HEAD_DIM), lambda b, i_r, p_r: (b, 0, 0))]
            + [blk_spec(g, kk) for g in range(N_KV) for kk in range(n_sel)],
            out_specs=pl.BlockSpec((1, N_KV, 8, HEAD_DIM), lambda b, i_r, p_r: (b, 0, 0, 0))),
        compiler_params=_cparams(1),
        name="sel_sample",
    )(idx.reshape(-1), page_table.reshape(-1), q4, new4, *([cache] * (N_KV * n_sel)))


def _win_sample_kernel(q_ref, win_ref, new_ref, oc_ref, os_ref, g_ref, o_ref, nw_ref, *, w_buf):
    n_col = 2 * N_KV
    new = new_ref[0]
    i = lax.broadcasted_iota(jnp.int32, (8, w_buf), 1)
    valid = (w_buf - i >= 0) & (w_buf - i < WINDOW)
    gate = jax.nn.sigmoid(g_ref[0])
    lane_g = lax.broadcasted_iota(jnp.int32, gate.shape, 1)
    row8 = lax.broadcasted_iota(jnp.int32, (8, HEAD_DIM), 0)
    outs = []
    for g in range(N_KV):
        q = q_ref[0, g]
        k = win_ref[0, pl.ds(g, w_buf, stride=n_col), :].astype(BF)
        v = win_ref[0, pl.ds(N_KV + g, w_buf, stride=n_col), :].astype(BF)
        k_new = new[g:g + 1].astype(BF).astype(F32)
        v_new = new[N_KV + g:N_KV + g + 1].astype(BF).astype(F32)
        s = jnp.where(valid, _dot_nt(q, k) * SCALE, NEG)
        s_new = jnp.sum(q.astype(F32) * k_new, axis=-1, keepdims=True) * SCALE
        m = jnp.maximum(jnp.max(s, axis=-1, keepdims=True), s_new)
        p = jnp.where(valid, jnp.exp(s - m), 0.0)
        p_new = jnp.exp(s_new - m)
        den = jnp.sum(p, axis=-1, keepdims=True) + p_new
        p = p / den
        p_new = p_new / den
        o_w = _dot(p.astype(BF), v) + p_new.astype(BF).astype(F32) * v_new
        o_c, o_s = oc_ref[0, g], os_ref[0, g]
        comb = jnp.zeros((8, HEAD_DIM), F32)
        for h in range(GROUP_R):
            base = (g * GROUP_R + h) * 3
            g3 = [jnp.sum(jnp.where(lane_g == base + bb, gate, 0.0), axis=-1, keepdims=True)
                  for bb in range(3)]
            mix = g3[0] * o_c + g3[1] * o_s + g3[2] * o_w
            comb = jnp.where(row8 == h, mix, comb)
        outs.append(comb)
    o_ref[0] = jnp.concatenate(outs, axis=0).astype(o_ref.dtype)
    n_rows = w_buf * n_col
    nw_ref[0] = pltpu.roll(win_ref[0], n_rows - n_col, 0)
    nw_ref[0, n_rows - n_col:, :] = new


def _win_sample(q_nsa, cache_win, new_win, o_c, o_s, g_nsa):
    bsz, w_buf = cache_win.shape[0], cache_win.shape[1]
    n_col = 2 * N_KV
    q4 = q_nsa.reshape(bsz, N_KV, GROUP_R, HEAD_DIM)
    q4 = jnp.pad(q4, ((0, 0), (0, 0), (0, 8 - GROUP_R), (0, 0)))
    o, nw = pl.pallas_call(
        functools.partial(_win_sample_kernel, w_buf=w_buf),
        out_shape=[jax.ShapeDtypeStruct((bsz, 16, HEAD_DIM), BF),
                   jax.ShapeDtypeStruct((bsz, w_buf * n_col, HEAD_DIM), F32)],
        grid=(bsz,),
        in_specs=[pl.BlockSpec((1, N_KV, 8, HEAD_DIM), lambda b: (b, 0, 0, 0)),
                  pl.BlockSpec((1, w_buf * n_col, HEAD_DIM), lambda b: (b, 0, 0)),
                  pl.BlockSpec((1, n_col, HEAD_DIM), lambda b: (b, 0, 0)),
                  pl.BlockSpec((1, N_KV, 8, HEAD_DIM), lambda b: (b, 0, 0, 0)),
                  pl.BlockSpec((1, N_KV, 8, HEAD_DIM), lambda b: (b, 0, 0, 0)),
                  pl.BlockSpec((1, 1, HEAD_DIM), lambda b: (b, 0, 0))],
        out_specs=[pl.BlockSpec((1, 16, HEAD_DIM), lambda b: (b, 0, 0)),
                   pl.BlockSpec((1, w_buf * n_col, HEAD_DIM), lambda b: (b, 0, 0))],
        compiler_params=_cparams(1),
        name="win_sample",
    )(q4, cache_win.reshape(bsz, w_buf * n_col, HEAD_DIM), new_win.reshape(bsz, n_col, HEAD_DIM),
      o_c, o_s, g_nsa.reshape(bsz, 1, HEAD_DIM))
    o = o.reshape(bsz, N_KV, 8, HEAD_DIM)[:, :, :GROUP_R].reshape(bsz, N_H_NSA * HEAD_DIM)
    return o, nw


def _branch_kernel(on_ref, os_ref, wn_ref, ws_ref, mn_ref, ms_ref, o_ref):
    y = (jax.nn.sigmoid(mn_ref[...]) * _dot(on_ref[...], wn_ref[...])
         + jax.nn.sigmoid(ms_ref[...]) * _dot(os_ref[...], ws_ref[...]))
    o_ref[...] = y.astype(o_ref.dtype)


def _branch_mix(o_nsa, o_sb, w_br_nsa, w_br_sb, m_all, tm):
    m, k = o_nsa.shape
    d = w_br_nsa.shape[1]
    tn = 512
    nj = d // tn
    return pl.pallas_call(
        _branch_kernel,
        out_shape=jax.ShapeDtypeStruct((m, d), BF),
        grid=(nj, m // tm),
        in_specs=[pl.BlockSpec((tm, k), lambda j, i: (i, 0)),
                  pl.BlockSpec((tm, k), lambda j, i: (i, 0)),
                  pl.BlockSpec((k, tn), lambda j, i: (0, j)),
                  pl.BlockSpec((k, tn), lambda j, i: (0, j)),
                  pl.BlockSpec((tm, tn), lambda j, i: (i, j)),
                  pl.BlockSpec((tm, tn), lambda j, i: (i, nj + j))],
        out_specs=pl.BlockSpec((tm, tn), lambda j, i: (i, j)),
        compiler_params=_cparams(2),
        name="branch_mix",
    )(o_nsa, o_sb, w_br_nsa, w_br_sb, m_all, m_all)


def _rms(x):
    return x * lax.rsqrt(jnp.mean(x * x, axis=-1, keepdims=True) + EPS)


def _post_mix_kernel(y_ref, w_ref, x_ref, gate_ref, gpost_ref, gpre_ref, sc_ref, sh_ref,
                     x1_ref, h_ref):
    y2 = _dot(y_ref[...], w_ref[...])
    x1 = x_ref[...] + _rows(gate_ref) * (_rms(y2) * gpost_ref[...])
    x1_ref[...] = x1
    h_ref[...] = (_rms(x1) * gpre_ref[...] * (1.0 + _rows(sc_ref))
                  + _rows(sh_ref)).astype(h_ref.dtype)


def _mod_spec2(tm, d, k, row_of_tile):
    if row_of_tile is None:
        return pl.BlockSpec((tm, d), lambda i: (i, k))
    return pl.BlockSpec((1, 1, d), lambda i: (row_of_tile(i), 0, k))


def _post_mix(y, w_out, x2d, mod, g_post_mix, g_pre_ffn, tm, row_of_tile):
    m, d = x2d.shape
    vec = lambda: pl.BlockSpec((1, d), lambda i: (0, 0))
    return pl.pallas_call(
        _post_mix_kernel,
        out_shape=[jax.ShapeDtypeStruct((m, d), F32), jax.ShapeDtypeStruct((m, d), BF)],
        grid=(m // tm,),
        in_specs=[pl.BlockSpec((tm, d), lambda i: (i, 0)),
                  pl.BlockSpec((d, d), lambda i: (0, 0)),
                  pl.BlockSpec((tm, d), lambda i: (i, 0)),
                  _mod_spec2(tm, d, 2, row_of_tile), vec(), vec(),
                  _mod_spec2(tm, d, 4, row_of_tile), _mod_spec2(tm, d, 3, row_of_tile)],
        out_specs=[pl.BlockSpec((tm, d), lambda i: (i, 0)),
                   pl.BlockSpec((tm, d), lambda i: (i, 0))],
        compiler_params=_cparams(1),
        name="post_mix",
    )(y, w_out, x2d, mod, g_post_mix.reshape(1, d), g_pre_ffn.reshape(1, d), mod, mod)


def _ffn_kernel(h_ref, wg_ref, wu_ref, wd_ref, x1_ref, gate_ref, gpost_ref, o_ref, acc_ref):
    j = pl.program_id(1)

    @pl.when(j == 0)
    def _():
        acc_ref[...] = jnp.zeros_like(acc_ref)

    h = h_ref[...]
    a = _dot(h, wg_ref[...])
    act = (a * jax.nn.sigmoid(a) * _dot(h, wu_ref[...])).astype(BF)
    acc_ref[...] += _dot(act, wd_ref[...])

    @pl.when(j == pl.num_programs(1) - 1)
    def _():
        o_ref[...] = x1_ref[...] + _rows(gate_ref) * (_rms(acc_ref[...]) * gpost_ref[...])


def _ffn(h2, wg, wu, wd, x1, mod, g_post_ffn, tm, row_of_tile):
    m, d = x1.shape
    dff = wg.shape[1]
    tf = 512
    if row_of_tile is None:
        gate_spec = pl.BlockSpec((tm, d), lambda i, j: (i, 5))
    else:
        gate_spec = pl.BlockSpec((1, 1, d), lambda i, j: (row_of_tile(i), 0, 5))
    return pl.pallas_call(
        _ffn_kernel,
        out_shape=jax.ShapeDtypeStruct((m, d), F32),
        grid=(m // tm, dff // tf),
        in_specs=[pl.BlockSpec((tm, d), lambda i, j: (i, 0)),
                  pl.BlockSpec((d, tf), lambda i, j: (0, j)),
                  pl.BlockSpec((d, tf), lambda i, j: (0, j)),
                  pl.BlockSpec((tf, d), lambda i, j: (j, 0)),
                  pl.BlockSpec((tm, d), lambda i, j: (i, 0)),
                  gate_spec,
                  pl.BlockSpec((1, d), lambda i, j: (0, 0))],
        out_specs=pl.BlockSpec((tm, d), lambda i, j: (i, 0)),
        scratch_shapes=[pltpu.VMEM((tm, d), F32)],
        compiler_params=_cparams(2),
        name="ffn",
    )(h2, wg, wu, wd, x1, mod, g_post_ffn.reshape(1, d))


def _split_w_in(w_in, d_model):
    q = N_H_NSA * HEAD_DIM
    kv = N_KV * HEAD_DIM
    sb = N_H_SB * HEAD_DIM
    gn = N_H_NSA * 3
    o = 0
    cols = {}
    for name, width in (("q_nsa", q), ("cmp", 2 * kv), ("sel", 2 * kv), ("win", 2 * kv),
                        ("g", gn), ("q_sb", sb), ("kv_sb", 2 * sb), ("m", 2 * d_model)):
        cols[name] = w_in[:, o:o + width].astype(BF)
        o += width
    cols["g"] = jnp.pad(cols["g"], ((0, 0), (0, HEAD_DIM - gn)))
    return cols


def _group(x2d, mod, row_of_tile, tm, pos, w, tail_w):
    g_pre_mix = tail_w["g_pre_mix"]
    h = _prenorm(x2d, g_pre_mix, mod, 1, 0, tm, row_of_tile)
    cos, sin = _rope_tables(pos)
    proj = {}
    proj["q_nsa"], = _project(h, w["q_nsa"], cos, sin, tm, 4, (BF,))
    proj["cmp"], = _project(h, w["cmp"], cos, sin, tm, 0, (), cache_rows=True)
    proj["sel"], proj["sel_bf"] = _project(h, w["sel"], cos, sin, tm, N_KV, (BF,), cache_rows=True)
    proj["win"], proj["win_bf"] = _project(h, w["win"], cos, sin, tm, N_KV, (BF,), cache_rows=True)
    proj["g"], = _project(h, w["g"], cos, sin, tm, 0, (F32,))
    proj["q_sb"], = _project(h, w["q_sb"], cos, sin, tm, 0, (BF,))
    proj["kv_sb"], proj["kv_sb_bf"] = _project(h, w["kv_sb"], cos, sin, tm, 0, (BF,),
                                               cache_rows=True)
    proj["m"], = _project(h, w["m"], cos, sin, tm, 0, (F32,))
    return proj


def _tail(x2d, o_nsa, o_sb, m_all, mod, row_of_tile, tm, tw):
    y = _branch_mix(o_nsa, o_sb, tw["w_br_nsa"], tw["w_br_sb"], m_all, tm)
    x1, h2 = _post_mix(y, tw["w_out"], x2d, mod, tw["g_post_mix"], tw["g_pre_ffn"], tm,
                       row_of_tile)
    return _ffn(h2, tw["w_ffn_gate"], tw["w_ffn_up"], tw["w_ffn_down"], x1, mod,
                tw["g_post_ffn"], tm, row_of_tile)


def kernel(x_prompt, x_sample, c_prompt, c_sample, cache_cmp, cache_sel, cache_sb, cache_win,
           page_table, w_ada, b_ada, g_pre_mix, g_post_mix, g_pre_ffn, g_post_ffn, w_in,
           pe_ck, w_ck1, b_ck1, w_ck2, pe_cv, w_cv1, b_cv1, w_cv2,
           w_br_nsa, w_br_sb, w_out, w_ffn_gate, w_ffn_up, w_ffn_down):
    n_p, t_p, d = x_prompt.shape
    n_s, t_s, _ = x_sample.shape
    assert t_s == 1
    n_pages = page_table.shape[1]
    past = n_pages * PAGE
    w_buf = cache_win.shape[1]
    kvw = 2 * N_KV * HEAD_DIM

    w = _split_w_in(w_in, d)
    tw = dict(g_pre_mix=g_pre_mix, g_post_mix=g_post_mix, g_pre_ffn=g_pre_ffn,
              g_post_ffn=g_post_ffn, w_br_nsa=w_br_nsa.astype(BF), w_br_sb=w_br_sb.astype(BF),
              w_out=w_out.astype(BF), w_ffn_gate=w_ffn_gate.astype(BF),
              w_ffn_up=w_ffn_up.astype(BF), w_ffn_down=w_ffn_down.astype(BF))
    cmp_w = _cmp_weights(pe_ck, w_ck1, b_ck1, w_ck2, pe_cv, w_cv1, b_cv1, w_cv2)

    n_mod = -(-(n_s + n_p) // 16) * 16
    c_all = jnp.concatenate([c_sample, c_prompt, jnp.zeros((n_mod - n_s - n_p, d), F32)])
    mod = _adaln(c_all, w_ada, b_ada)
    mod3 = mod.reshape(n_mod, 1, 6 * d)

    tm_p = 512
    tiles_per_seq = t_p // tm_p
    row_p = lambda i: n_s + i // tiles_per_seq
    xp = x_prompt.reshape(n_p * t_p, d)
    pp = _group(xp, mod3, row_p, tm_p, jnp.arange(t_p), w, tw)
    kvc = _compress_prompt(pp["cmp"], cmp_w, n_p, t_p)
    tq = 256
    o_nsa = _nsa_prompt(pp["q_nsa"], kvc, pp["sel_bf"], pp["win_bf"], pp["g"], n_p, t_p, tq)
    o_sb = _sb_prompt(pp["q_sb"], pp["kv_sb_bf"], n_p, t_p, tq)
    y_prompt = _tail(xp, o_nsa, o_sb, pp["m"], mod3, row_p, tm_p, tw).reshape(n_p, t_p, d)
    new_cmp_p = pp["cmp"].reshape(n_p, t_p, 2, N_KV, HEAD_DIM)
    new_sel_p = pp["sel"].reshape(n_p, t_p, 2, N_KV, HEAD_DIM)
    new_sb_p = pp["kv_sb"].reshape(n_p, t_p, 2, N_H_SB, HEAD_DIM)
    win_rows = pp["win"].reshape(n_p, t_p, 2, N_KV, HEAD_DIM)
    if t_p >= w_buf:
        new_win_p = win_rows[:, t_p - w_buf:]
    else:
        new_win_p = jnp.pad(win_rows, ((0, 0), (w_buf - t_p, 0), (0, 0), (0, 0), (0, 0)))

    xs = x_sample.reshape(n_s, d)
    pos_s = jnp.full((n_s,), past, jnp.int32)
    ps = _group(xs, mod, None, n_s, pos_s, w, tw)
    o_c, idx = _cmp_sample(ps["q_nsa"], cache_cmp, page_table, cmp_w, past)
    idx = idx[:, :, 0, :min(N_SELECT, -(-(past + 1) // SEL_BLOCK))]
    o_s = _sel_sample(ps["q_nsa"], idx, ps["sel"], cache_sel, page_table)
    o_nsa_s, new_win_s = _win_sample(ps["q_nsa"], cache_win, ps["win"], o_c, o_s, ps["g"])
    o_sb_s = _sb_sample(ps["q_sb"], cache_sb, page_table)
    y_sample = _tail(xs, o_nsa_s, o_sb_s, ps["m"], mod, None, n_s, tw).reshape(n_s, 1, d)

    return (y_prompt, y_sample, new_cmp_p, new_sel_p, new_sb_p, new_win_p,
            ps["cmp"].reshape(n_s, 1, 2, N_KV, HEAD_DIM),
            ps["sel"].reshape(n_s, 1, 2, N_KV, HEAD_DIM),
            ps["kv_sb"].reshape(n_s, 1, 2, N_H_SB, HEAD_DIM),
            new_win_s.reshape(n_s, w_buf, 2, N_KV, HEAD_DIM))
```

```python
import functools

import numpy as np
import jax
import jax.numpy as jnp
from jax import lax
from jax.experimental import pallas as pl
from jax.experimental.pallas import tpu as pltpu

BF = jnp.bfloat16
F32 = jnp.float32

HEAD_DIM = 128
N_KV = 2
GROUP_R = 4
N_H_NSA = N_KV * GROUP_R
N_H_SB = 8
L_CMP = 32
STRIDE = 16
SEL_BLOCK = 64
N_SELECT = 16
WINDOW = 512
PAGE = 128
CMP_HIDDEN = 256
ROPE_THETA = 10000.0
EPS = 1e-6
SCALE = HEAD_DIM ** -0.5
NEG = -1e30
VMEM_LIMIT = 56 * 1024 * 1024


def _cparams(n_axes):
    return pltpu.CompilerParams(dimension_semantics=("arbitrary",) * n_axes,
                                vmem_limit_bytes=VMEM_LIMIT)


def _dot(a, b):
    return jnp.dot(a, b, preferred_element_type=F32)


def _dot_nt(a, b):
    return lax.dot_general(a, b, (((1,), (1,)), ((), ())), preferred_element_type=F32)


def _split3(x):
    hi = x.astype(BF)
    r = x - hi.astype(F32)
    mid = r.astype(BF)
    lo = (r - mid.astype(F32)).astype(BF)
    return hi, mid, lo


def _softmax_rows(s, valid):
    s = jnp.where(valid, s, NEG)
    m = jnp.max(s, axis=-1, keepdims=True)
    m = jnp.where(m > 0.5 * NEG, m, 0.0)
    p = jnp.where(valid, jnp.exp(s - m), 0.0)
    return p / jnp.maximum(jnp.sum(p, axis=-1, keepdims=True), 1e-30)


def _topk_select(val, k, lane):
    big = jnp.int32(1 << 30)
    sel = jnp.zeros(val.shape, F32)
    picks = []
    for _ in range(k):
        m = jnp.max(val, axis=-1, keepdims=True)
        idx = jnp.min(jnp.where(val == m, lane, big), axis=-1, keepdims=True)
        hit = lane == idx
        sel = jnp.where(hit, 1.0, sel)
        val = jnp.where(hit, -2.0, val)
        picks.append(idx)
    return sel, picks


def _block_scores(imp, mmat_ref):
    hi, mid, lo = _split3(imp)
    m = mmat_ref[...]
    return _dot(hi, m) + _dot(mid, m) + _dot(lo, m)


def _rope_tables(pos):
    half = HEAD_DIM // 2
    inv = ROPE_THETA ** (-2.0 * jnp.arange(half, dtype=F32) / HEAD_DIM)
    ang = pos.astype(F32)[:, None] * inv[None, :]
    c, s = jnp.cos(ang), jnp.sin(ang)
    return jnp.concatenate([c, c], axis=1), jnp.concatenate([-s, s], axis=1)


def _rope(x, cos, sin_signed):
    return x * cos + pltpu.roll(x, HEAD_DIM // 2, 1) * sin_signed


def _score_matrix(n_c_pad, n_blk_pad):
    j = np.arange(n_c_pad)[:, None]
    b = np.arange(n_blk_pad)[None, :]
    spb = SEL_BLOCK // STRIDE
    m = ((j == spb * b - 1).astype(np.float32) + (j == spb * b + spb - 1)
         + 2.0 * ((j >= spb * b) & (j <= spb * b + spb - 2)))
    return jnp.asarray(m, BF)


def _ada_kernel(c_ref, w_ref, b_ref, o_ref):
    o_ref[...] = _dot(c_ref[...], w_ref[...].astype(BF)) + b_ref[...]


def _adaln(c_pad, w_ada, b_ada):
    m, d = c_pad.shape
    n = w_ada.shape[1]
    tn = 1536
    return pl.pallas_call(
        _ada_kernel,
        out_shape=jax.ShapeDtypeStruct((m, n), F32),
        grid=(n // tn,),
        in_specs=[pl.BlockSpec((m, d), lambda j: (0, 0)),
                  pl.BlockSpec((d, tn), lambda j: (0, j)),
                  pl.BlockSpec((1, tn), lambda j: (0, j))],
        out_specs=pl.BlockSpec((m, tn), lambda j: (0, j)),
        compiler_params=_cparams(1),
        name="adaln",
    )(c_pad.astype(BF), w_ada, b_ada.reshape(1, n))


def _rows(ref):
    v = ref[...]
    return v.reshape(v.shape[-2], v.shape[-1])


def _prenorm_kernel(x_ref, g_ref, sc_ref, sh_ref, o_ref):
    x = x_ref[...]
    y = x * lax.rsqrt(jnp.mean(x * x, axis=-1, keepdims=True) + EPS)
    o_ref[...] = (y * g_ref[...] * (1.0 + _rows(sc_ref)) + _rows(sh_ref)).astype(o_ref.dtype)


def _mod_spec(tm, d, k, row_of_tile):
    if row_of_tile is None:
        return pl.BlockSpec((tm, d), lambda i: (i, k))
    return pl.BlockSpec((1, 1, d), lambda i: (row_of_tile(i), 0, k))


def _prenorm(x2d, g, mod, k_scale, k_shift, tm, row_of_tile):
    m, d = x2d.shape
    return pl.pallas_call(
        _prenorm_kernel,
        out_shape=jax.ShapeDtypeStruct((m, d), BF),
        grid=(m // tm,),
        in_specs=[pl.BlockSpec((tm, d), lambda i: (i, 0)),
                  pl.BlockSpec((1, d), lambda i: (0, 0)),
                  _mod_spec(tm, d, k_scale, row_of_tile),
                  _mod_spec(tm, d, k_shift, row_of_tile)],
        out_specs=pl.BlockSpec((tm, d), lambda i: (i, 0)),
        compiler_params=_cparams(1),
        name="prenorm",
    )(x2d, g.reshape(1, d), mod, mod)


def _proj_kernel(h_ref, w_ref, cos_ref, sin_ref, *o_refs, rope_heads, tn, cache_rows):
    y = _dot(h_ref[...], w_ref[...])
    n_head = tn // HEAD_DIM
    parts = [y[:, hd * HEAD_DIM:(hd + 1) * HEAD_DIM] for hd in range(n_head)]
    if rope_heads:
        cos, sin = cos_ref[...], sin_ref[...]
        parts = [_rope(p, cos, sin) if hd < rope_heads else p for hd, p in enumerate(parts)]
        y = jnp.concatenate(parts, axis=1)
    if cache_rows:
        tm = y.shape[0]
        for hd, p in enumerate(parts):
            o_refs[0][pl.ds(hd, tm, stride=n_head), :] = p
        o_refs = o_refs[1:]
    for o_ref in o_refs:
        o_ref[...] = y.astype(o_ref.dtype)


def _project(h, w, cos, sin, tm, rope_heads, out_dtypes, cache_rows=False):
    m, d = h.shape
    n = w.shape[1]
    tn = n if cache_rows else min(n, 1024)
    t_tiles = cos.shape[0] // tm
    n_head = n // HEAD_DIM
    out_shape = [jax.ShapeDtypeStruct((m, n), dt) for dt in out_dtypes]
    out_specs = [pl.BlockSpec((tm, tn), lambda j, i: (i, j)) for _ in out_dtypes]
    if cache_rows:
        out_shape.insert(0, jax.ShapeDtypeStruct((m * n_head, HEAD_DIM), F32))
        out_specs.insert(0, pl.BlockSpec((tm * n_head, HEAD_DIM), lambda j, i: (i, 0)))
    outs = pl.pallas_call(
        functools.partial(_proj_kernel, rope_heads=rope_heads, tn=tn, cache_rows=cache_rows),
        out_shape=out_shape,
        grid=(n // tn, m // tm),
        in_specs=[pl.BlockSpec((tm, d), lambda j, i: (i, 0)),
                  pl.BlockSpec((d, tn), lambda j, i: (0, j)),
                  pl.BlockSpec((tm, HEAD_DIM), lambda j, i: (i % t_tiles, 0)),
                  pl.BlockSpec((tm, HEAD_DIM), lambda j, i: (i % t_tiles, 0))],
        out_specs=out_specs,
        compiler_params=_cparams(2),
        name="project",
    )(h, w, cos, sin)
    return outs


def _gelu(x):
    return 0.5 * x * (1.0 + jnp.tanh(0.7978845608028654 * (x + 0.044715 * x * x * x)))


def _cmp_bias(pe_ref, w1_ref, b1_ref):
    pb = _dot(pe_ref[0], w1_ref[0])
    return pb[0:1, :CMP_HIDDEN] + pb[1:2, CMP_HIDDEN:] + b1_ref[0]


def _cmp_finish(first, second, bias, w2, cos, sin, is_key, nseg):
    nxt = pltpu.roll(second, nseg - 1, 0)
    hid = _gelu(first + nxt + bias).astype(BF)
    out = _dot(hid, w2)
    out = jnp.where(is_key, _rope(out, cos, sin), out)
    row = lax.broadcasted_iota(jnp.int32, out.shape, 0)
    return jnp.where(row < nseg - 1, out, 0.0)


def _cmp_prompt_kernel(x_ref, w1_ref, pe_ref, b1_ref, w2_ref, cos_ref, sin_ref, o_ref, *, nseg):
    kv = pl.program_id(1)
    n_col = 2 * N_KV
    bias = _cmp_bias(pe_ref, w1_ref, b1_ref)
    for g in range(N_KV):
        segs = []
        for c in (g, N_KV + g):
            segs.append(jnp.concatenate(
                [x_ref[0, pl.ds(r * n_col + c, nseg, stride=STRIDE * n_col), :]
                 for r in range(STRIDE)], axis=1))
        seg = jnp.where(kv == 0, segs[0], segs[1]).astype(BF)
        fs = _dot(seg, w1_ref[0])
        out = _cmp_finish(fs[:, :CMP_HIDDEN], fs[:, CMP_HIDDEN:], bias, w2_ref[0],
                          cos_ref[...], sin_ref[...], kv == 0, nseg)
        o_ref[0, g] = out.astype(o_ref.dtype)


def _cmp_weights(pe_ck, w_ck1, b_ck1, w_ck2, pe_cv, w_cv1, b_cv1, w_cv2):
    half = STRIDE * HEAD_DIM

    def w1cat(w):
        return jnp.concatenate([w[:half], w[half:]], axis=1)

    w1 = jnp.stack([w1cat(w_ck1), w1cat(w_cv1)]).astype(BF)
    pe = jnp.stack([pe_ck.reshape(2, half), pe_cv.reshape(2, half)])
    pe = jnp.pad(pe, ((0, 0), (0, 6), (0, 0))).astype(BF)
    b1 = jnp.stack([b_ck1, b_cv1]).reshape(2, 1, CMP_HIDDEN)
    w2 = jnp.stack([w_ck2, w_cv2]).astype(BF)
    return w1, pe, b1, w2


def _compress_prompt(kv_cmp, cmp_w, n, t):
    w1, pe, b1, w2 = cmp_w
    nseg = t // STRIDE
    n_col = 2 * N_KV
    cos, sin = _rope_tables(jnp.arange(nseg) * STRIDE)
    x3 = kv_cmp.reshape(n, t * n_col, HEAD_DIM)
    return pl.pallas_call(
        functools.partial(_cmp_prompt_kernel, nseg=nseg),
        out_shape=jax.ShapeDtypeStruct((n, n_col, nseg, HEAD_DIM), BF),
        grid=(n, 2),
        in_specs=[pl.BlockSpec((1, t * n_col, HEAD_DIM), lambda b, kv: (b, 0, 0)),
                  pl.BlockSpec((1,) + w1.shape[1:], lambda b, kv: (kv, 0, 0)),
                  pl.BlockSpec((1,) + pe.shape[1:], lambda b, kv: (kv, 0, 0)),
                  pl.BlockSpec((1,) + b1.shape[1:], lambda b, kv: (kv, 0, 0)),
                  pl.BlockSpec((1,) + w2.shape[1:], lambda b, kv: (kv, 0, 0)),
                  pl.BlockSpec((nseg, HEAD_DIM), lambda b, kv: (0, 0)),
                  pl.BlockSpec((nseg, HEAD_DIM), lambda b, kv: (0, 0))],
        out_specs=pl.BlockSpec((1, N_KV, nseg, HEAD_DIM), lambda b, kv: (b, kv, 0, 0)),
        compiler_params=_cparams(2),
        name="compress_prompt",
    )(x3, w1, pe, b1, w2, cos, sin)


def _flash_step(qs, k, v, valid, m, l, acc):
    tq, tk = valid.shape
    s = (_dot_nt(qs, k) * SCALE).reshape(GROUP_R, tq, tk)
    s = jnp.where(valid[None], s, NEG).reshape(GROUP_R * tq, tk)
    m_new = jnp.maximum(m, jnp.max(s, axis=-1, keepdims=True))
    alpha = jnp.exp(m - m_new)
    p = jnp.exp(s - m_new)
    l = alpha * l + jnp.sum(p, axis=-1, keepdims=True)
    acc = alpha * acc + _dot(p.astype(BF), v)
    return m_new, l, acc


def _rank_select_t(val_t, k, n_blk):
    blk = lax.broadcasted_iota(jnp.int32, val_t.shape, 0)
    cnt = jnp.zeros(val_t.shape, F32)
    for other in range(n_blk):
        vo = val_t[other:other + 1, :]
        beats = (vo > val_t) | ((vo == val_t) & (blk > other))
        cnt = cnt + jnp.where(beats, 1.0, 0.0)
    return jnp.where((cnt < k) & (val_t >= 0.0), 1.0, 0.0)


def _nsa_prompt_kernel(q_ref, kc_ref, vc_ref, ks_ref, vs_ref, kw_ref, vw_ref, g_ref, mm_ref,
                       o_ref, *, tq, n_c, n_blk):
    g_idx = pl.program_id(1)
    qi = pl.program_id(2)
    q = q_ref[...]
    qs = jnp.concatenate([q[:, h * HEAD_DIM:(h + 1) * HEAD_DIM] for h in range(GROUP_R)], axis=0)
    rows = GROUP_R * tq
    qpos = qi * tq + lax.broadcasted_iota(jnp.int32, (tq, 1), 0)

    n_cp = kc_ref.shape[2]
    s = (_dot_nt(qs, kc_ref[0, 0]) * SCALE).reshape(GROUP_R, tq, n_cp)
    jc = lax.broadcasted_iota(jnp.int32, (tq, n_cp), 1)
    valid_c = (jc * STRIDE + (L_CMP - 1) <= qpos) & (jc < n_c)
    p_c = _softmax_rows(s, valid_c[None])
    o_c = _dot(p_c.reshape(rows, n_cp).astype(BF), vc_ref[0, 0])
    imp = jnp.sum(p_c, axis=0)

    mm_t = mm_ref[...]
    n_lane = mm_t.shape[0]
    score_t = sum(_dot_nt(mm_t, part) for part in _split3(imp))
    blk = lax.broadcasted_iota(jnp.int32, (n_lane, tq), 0)
    cur = (qi * tq + lax.broadcasted_iota(jnp.int32, (1, tq), 1)) // SEL_BLOCK
    forced = (blk == 0) | (blk == cur) | (blk == cur - 1)
    val_t = jnp.where(blk > cur, -1.0, jnp.where(forced, 3e38, score_t))
    sel_t = _rank_select_t(val_t[:n_blk], min(N_SELECT, n_blk), n_blk)
    if n_blk < n_lane:
        sel_t = jnp.concatenate([sel_t, jnp.zeros((n_lane - n_blk, tq), F32)], axis=0)
    sel = sel_t.T.astype(BF)

    lane_k = lax.broadcasted_iota(jnp.int32, (tq, tq), 1)
    eb = lax.broadcasted_iota(jnp.int32, (n_lane, tq), 0)
    ej = lax.broadcasted_iota(jnp.int32, (n_lane, tq), 1) // SEL_BLOCK
    init = (jnp.full((rows, 1), 0.5 * NEG, F32), jnp.zeros((rows, 1), F32),
            jnp.zeros((rows, HEAD_DIM), F32))

    def sel_body(kt, carry):
        k0 = pl.multiple_of(kt * tq, tq)
        expand = jnp.where(eb == ej + kt * (tq // SEL_BLOCK), 1.0, 0.0).astype(BF)
        chosen = _dot(sel, expand) > 0.5
        valid = chosen & (k0 + lane_k <= qpos)
        return _flash_step(qs, ks_ref[pl.ds(k0, tq), :], vs_ref[pl.ds(k0, tq), :], valid, *carry)

    _, l_s, acc_s = lax.fori_loop(0, qi + 1, sel_body, init)
    o_s = acc_s / jnp.maximum(l_s, 1e-30)

    def win_body(kt, carry):
        k0 = pl.multiple_of(kt * tq, tq)
        dist = qpos - (k0 + lane_k)
        valid = (dist >= 0) & (dist < WINDOW)
        return _flash_step(qs, kw_ref[pl.ds(k0, tq), :], vw_ref[pl.ds(k0, tq), :], valid, *carry)

    _, l_w, acc_w = lax.fori_loop(jnp.maximum(qi - WINDOW // tq, 0), qi + 1, win_body, init)
    o_w = acc_w / jnp.maximum(l_w, 1e-30)

    gate = jax.nn.sigmoid(g_ref[...])
    lane_g = lax.broadcasted_iota(jnp.int32, gate.shape, 1)
    outs = []
    for h in range(GROUP_R):
        base = (g_idx * GROUP_R + h) * 3
        g3 = [jnp.sum(jnp.where(lane_g == base + b, gate, 0.0), axis=-1, keepdims=True)
              for b in range(3)]
        sl = slice(h * tq, (h + 1) * tq)
        outs.append(g3[0] * o_c[sl] + g3[1] * o_s[sl] + g3[2] * o_w[sl])
    o_ref[...] = jnp.concatenate(outs, axis=1).astype(o_ref.dtype)


def _nsa_prompt(q_nsa, kvc, sel_bf, win_bf, g_nsa, n, t, tq):
    nseg = kvc.shape[2]
    n_c = nseg - 1
    n_blk = -(-t // SEL_BLOCK)
    qt = t // tq
    gw = GROUP_R * HEAD_DIM
    mmat = _score_matrix(nseg, 128).T
    kv_spec = lambda col: pl.BlockSpec((t, HEAD_DIM), lambda b, g, i: (b, col + g))
    return pl.pallas_call(
        functools.partial(_nsa_prompt_kernel, tq=tq, n_c=n_c, n_blk=n_blk),
        out_shape=jax.ShapeDtypeStruct((n * t, N_H_NSA * HEAD_DIM), BF),
        grid=(n, N_KV, qt),
        in_specs=[pl.BlockSpec((tq, gw), lambda b, g, i: (b * qt + i, g)),
                  pl.BlockSpec((1, 1, nseg, HEAD_DIM), lambda b, g, i: (b, g, 0, 0)),
                  pl.BlockSpec((1, 1, nseg, HEAD_DIM), lambda b, g, i: (b, N_KV + g, 0, 0)),
                  kv_spec(0), kv_spec(N_KV), kv_spec(0), kv_spec(N_KV),
                  pl.BlockSpec((tq, HEAD_DIM), lambda b, g, i: (b * qt + i, 0)),
                  pl.BlockSpec(mmat.shape, lambda b, g, i: (0, 0))],
        out_specs=pl.BlockSpec((tq, gw), lambda b, g, i: (b * qt + i, g)),
        compiler_params=_cparams(3),
        name="nsa_prompt",
    )(q_nsa, kvc, kvc, sel_bf, sel_bf, win_bf, win_bf, g_nsa, mmat)


def _softplus(z):
    return jnp.maximum(z, 0.0) + jnp.log1p(jnp.exp(-jnp.abs(z)))


def _strict_upper(n):
    return jnp.where(lax.broadcasted_iota(jnp.int32, (n, n), 0)
                     > lax.broadcasted_iota(jnp.int32, (n, n), 1), 1.0, 0.0).astype(BF)


def _suffix_sum(lm, u):
    hi = lm.astype(BF)
    lo = (lm - hi.astype(F32)).astype(BF)
    return _dot(hi, u) + _dot(lo, u)


SB_DEAD = -110.0


def _sb_tile(q, k, v, u, causal, run, acc):
    z = _dot_nt(q, k) * SCALE
    sp = _softplus(z)
    lm = -sp if causal is None else jnp.where(causal, -sp, 0.0)
    a = jnp.exp(z - sp + _suffix_sum(lm, u) + run)
    if causal is not None:
        a = jnp.where(causal, a, 0.0)
    acc = acc + _dot(a.astype(BF), v)
    return run + jnp.sum(lm, axis=-1, keepdims=True), acc


def _sb_prompt_kernel(q_ref, k_ref, v_ref, o_ref, *, tq, nh):
    qi = pl.program_id(2)
    u = _strict_upper(tq)
    qpos = qi * tq + lax.broadcasted_iota(jnp.int32, (tq, 1), 0)
    lane = lax.broadcasted_iota(jnp.int32, (tq, tq), 1)

    def cond(st):
        return (st[0] <= qi) & (st[1] > 0)

    def body(st):
        step = st[0]
        k0 = pl.multiple_of((qi - step) * tq, tq)
        causal = k0 + lane < qpos
        runs, accs, top = [], [], None
        for h in range(nh):
            cols = slice(h * HEAD_DIM, (h + 1) * HEAD_DIM)
            run, acc = _sb_tile(q_ref[:, cols], k_ref[pl.ds(k0, tq), cols],
                                v_ref[pl.ds(k0, tq), cols], u, causal, st[2 + h], st[2 + nh + h])
            runs.append(run)
            accs.append(acc)
            top = jnp.max(run) if top is None else jnp.maximum(top, jnp.max(run))
        return (step + 1, (top > SB_DEAD).astype(jnp.int32), *runs, *accs)

    init = ((jnp.int32(0), jnp.int32(1)) + tuple(jnp.zeros((tq, 1), F32) for _ in range(nh))
            + tuple(jnp.zeros((tq, HEAD_DIM), F32) for _ in range(nh)))
    st = lax.while_loop(cond, body, init)
    o_ref[...] = jnp.concatenate(st[2 + nh:], axis=1).astype(o_ref.dtype)


def _sb_prompt(q_sb, kv_sb_bf, n, t, tq):
    qt = t // tq
    nh = 4
    ng = N_H_SB // nh
    return pl.pallas_call(
        functools.partial(_sb_prompt_kernel, tq=tq, nh=nh),
        out_shape=jax.ShapeDtypeStruct((n * t, N_H_SB * HEAD_DIM), BF),
        grid=(n, ng, qt),
        in_specs=[pl.BlockSpec((tq, nh * HEAD_DIM), lambda b, h, i: (b * qt + i, h)),
                  pl.BlockSpec((t, nh * HEAD_DIM), lambda b, h, i: (b, h)),
                  pl.BlockSpec((t, nh * HEAD_DIM), lambda b, h, i: (b, ng + h))],
        out_specs=pl.BlockSpec((tq, nh * HEAD_DIM), lambda b, h, i: (b * qt + i, h)),
        compiler_params=_cparams(3),
        name="sb_prompt",
    )(q_sb, kv_sb_bf, kv_sb_bf)


def _sb_sample_kernel(pt_ref, q_ref, cache_ref, o_ref, buf, sem, *, n_pg, n_pages):
    b = pl.program_id(0)
    n_chunk = n_pages // n_pg
    rows_per_tok = 2 * N_H_SB

    def page_copy(chunk, slot, pg):
        page = pt_ref[b * n_pages + (n_chunk - 1 - chunk) * n_pg + pg]
        return pltpu.make_async_copy(cache_ref.at[page], buf.at[slot, pg], sem.at[slot, pg])

    def start(chunk, slot):
        for pg in range(n_pg):
            page_copy(chunk, slot, pg).start()

    def wait(chunk, slot):
        for pg in range(n_pg):
            page_copy(chunk, slot, pg).wait()

    start(0, 0)
    q = q_ref[0]
    u = _strict_upper(PAGE)
    row = lax.broadcasted_iota(jnp.int32, (N_H_SB, PAGE), 0)

    def cond(st):
        return (st[0] < n_chunk) & (st[1] > 0)

    def body(st):
        c, _, run, acc = st
        slot = c % 2
        wait(c, slot)

        @pl.when(c + 1 < n_chunk)
        def _():
            start(c + 1, 1 - slot)

        for pg in reversed(range(n_pg)):
            page = buf.at[slot, pg]
            z = jnp.zeros((N_H_SB, PAGE), F32)
            for h in range(N_H_SB):
                kh = page[pl.ds(h, PAGE, stride=rows_per_tok), :].astype(BF)
                z = jnp.where(row == h, _dot_nt(q, kh), z)
            z = z * SCALE
            sp = _softplus(z)
            lm = -sp
            a = jnp.exp(z - sp + _suffix_sum(lm, u) + run).astype(BF)
            for h in range(N_H_SB):
                vh = page[pl.ds(N_H_SB + h, PAGE, stride=rows_per_tok), :].astype(BF)
                acc = acc + jnp.where(row == h, _dot(a, vh), 0.0)
            run = run + jnp.sum(lm, axis=-1, keepdims=True)
        return c + 1, (jnp.max(run) > SB_DEAD).astype(jnp.int32), run, acc

    c, _, _, acc = lax.while_loop(
        cond, body, (jnp.int32(0), jnp.int32(1), jnp.zeros((N_H_SB, 1), F32),
                     jnp.zeros((N_H_SB, HEAD_DIM), F32)))

    @pl.when(c < n_chunk)
    def _():
        wait(c, c % 2)

    o_ref[0] = acc.astype(o_ref.dtype)


def _sb_sample(q_sb, cache_sb, page_table):
    bsz, n_pages = page_table.shape
    n_phys = cache_sb.shape[0]
    n_pg = 2
    hw = N_H_SB * HEAD_DIM
    page_rows = PAGE * 2 * N_H_SB
    cache = cache_sb.reshape(n_phys, page_rows, HEAD_DIM)
    out = pl.pallas_call(
        functools.partial(_sb_sample_kernel, n_pg=n_pg, n_pages=n_pages),
        out_shape=jax.ShapeDtypeStruct((bsz, N_H_SB, HEAD_DIM), BF),
        grid_spec=pltpu.PrefetchScalarGridSpec(
            num_scalar_prefetch=1,
            grid=(bsz,),
            in_specs=[pl.BlockSpec((1, N_H_SB, HEAD_DIM), lambda b, pt: (b, 0, 0)),
                      pl.BlockSpec(memory_space=pl.ANY)],
            out_specs=pl.BlockSpec((1, N_H_SB, HEAD_DIM), lambda b, pt: (b, 0, 0)),
            scratch_shapes=[pltpu.VMEM((2, n_pg, page_rows, HEAD_DIM), F32),
                            pltpu.SemaphoreType.DMA((2, n_pg))]),
        compiler_params=_cparams(1),
        name="sb_sample",
    )(page_table.reshape(-1), q_sb.reshape(bsz, N_H_SB, HEAD_DIM), cache)
    return out.reshape(bsz, hw)


def _sublane_transpose8(tiles):
    sub = lax.broadcasted_iota(jnp.int32, tiles[0].shape, 0)
    for d in (4, 2, 1):
        keep = (sub % (2 * d)) < d
        nxt = list(tiles)
        for s in range(8):
            if s % (2 * d) < d:
                a, b = tiles[s], tiles[s + d]
                nxt[s] = jnp.where(keep, a, pltpu.roll(b, d, 0))
                nxt[s + d] = jnp.where(keep, pltpu.roll(a, 8 - d, 0), b)
        tiles = nxt
    return tiles


def _cmp_sample_kernel(pt_ref, q_ref, w1_ref, pe_ref, b1_ref, w2_ref, cos_ref, sin_ref, mm_ref,
                       *refs, n_pg, nseg, n_blk, q_pos):
    page_refs = refs[:n_pg]
    oc_ref, idx_ref = refs[n_pg:n_pg + 2]
    fs_ref = refs[n_pg + 2]
    j = pl.program_id(1)
    spp = PAGE // STRIDE
    rows = n_pg * spp
    row0 = pl.multiple_of(j * rows, rows)
    n_col = 2 * N_KV
    seg_rows = STRIDE * n_col
    sorted_pages = []
    for pg in range(n_pg):
        by_rc = []
        for jp in range(seg_rows // 8):
            tiles = _sublane_transpose8(
                [page_refs[pg][0, s * seg_rows + jp * 8:s * seg_rows + (jp + 1) * 8, :]
                 for s in range(spp)])
            by_rc.extend(tiles)
        sorted_pages.append(by_rc)
    for kv in range(2):
        segs = []
        for c in range(kv * N_KV, (kv + 1) * N_KV):
            for pg in range(n_pg):
                segs.append(jnp.concatenate(
                    [sorted_pages[pg][r * n_col + c] for r in range(STRIDE)], axis=1))
        segs = jnp.concatenate(segs, axis=0).astype(BF)
        prod = _dot(segs, w1_ref[kv])
        for g in range(N_KV):
            fs_ref[kv * N_KV + g, pl.ds(row0, rows), :] = prod[g * rows:(g + 1) * rows]

    @pl.when(j == pl.num_programs(1) - 1)
    def _():
        n_c = nseg - 1
        cos, sin = cos_ref[...], sin_ref[...]
        comp = []
        for c in range(2 * N_KV):
            kv = c // N_KV
            pb = _dot(pe_ref[kv], w1_ref[kv])
            bias = pb[0:1, :CMP_HIDDEN] + pb[1:2, CMP_HIDDEN:] + b1_ref[kv]
            fs = fs_ref[c]
            comp.append(_cmp_finish(fs[:, :CMP_HIDDEN], fs[:, CMP_HIDDEN:], bias, w2_ref[kv],
                                    cos, sin, kv == 0, nseg).astype(BF))
        jc = lax.broadcasted_iota(jnp.int32, (8, nseg), 1)
        valid = (jc * STRIDE + (L_CMP - 1) <= q_pos) & (jc < n_c)
        head = lax.broadcasted_iota(jnp.int32, (8, nseg), 0)
        n_lane = mm_ref.shape[1]
        blk = lax.broadcasted_iota(jnp.int32, (8, n_lane), 1)
        cur = q_pos // SEL_BLOCK
        forced = (blk == 0) | (blk == cur) | (blk == cur - 1)
        lane_o = lax.broadcasted_iota(jnp.int32, (8, HEAD_DIM), 1)
        for g in range(N_KV):
            qg = q_ref[0, g]
            s = _dot_nt(qg, comp[g]) * SCALE
            p = _softmax_rows(s, valid)
            oc_ref[0, g] = _dot(p.astype(BF), comp[N_KV + g])
            imp = jnp.sum(jnp.where(head < GROUP_R, p, 0.0), axis=0, keepdims=True)
            score = _block_scores(jnp.broadcast_to(imp, (8, nseg)), mm_ref)
            val = jnp.where((blk > cur) | (blk >= n_blk), -1.0, jnp.where(forced, 3e38, score))
            _, picks = _topk_select(val, min(N_SELECT, n_blk), blk)
            row = jnp.zeros((8, HEAD_DIM), jnp.int32)
            for kk, pick in enumerate(picks):
                row = jnp.where(lane_o == kk, pick, row)
            idx_ref[0, g] = row


def _cmp_sample(q_nsa, cache_cmp, page_table, cmp_w, q_pos):
    w1, pe, b1, w2 = cmp_w
    bsz, n_pages = page_table.shape
    n_phys = cache_cmp.shape[0]
    n_pg = min(32, n_pages)
    n_chunk = n_pages // n_pg
    nseg = n_pages * PAGE // STRIDE
    n_blk = -(-(n_pages * PAGE + 1) // SEL_BLOCK)
    n_lane = -(-n_blk // 128) * 128
    mmat = _score_matrix(nseg, n_lane)
    cos, sin = _rope_tables(jnp.arange(nseg) * STRIDE)
    page_rows = PAGE * 2 * N_KV
    cache = cache_cmp.reshape(n_phys, page_rows, HEAD_DIM)
    q4 = q_nsa.reshape(bsz, N_KV, GROUP_R, HEAD_DIM)
    q4 = jnp.pad(q4, ((0, 0), (0, 0), (0, 8 - GROUP_R), (0, 0)))

    def page_spec(pg):
        return pl.BlockSpec((1, page_rows, HEAD_DIM),
                            lambda b, j, pt: (pt[b * n_pages + j * n_pg + pg], 0, 0))

    page_specs = [page_spec(pg) for pg in range(n_pg)]
    full = lambda a: pl.BlockSpec(a.shape, lambda b, j, pt: (0,) * a.ndim)
    return pl.pallas_call(
        functools.partial(_cmp_sample_kernel, n_pg=n_pg, nseg=nseg, n_blk=n_blk, q_pos=q_pos),
        out_shape=[jax.ShapeDtypeStruct((bsz, N_KV, 8, HEAD_DIM), F32),
                   jax.ShapeDtypeStruct((bsz, N_KV, 8, HEAD_DIM), jnp.int32)],
        grid_spec=pltpu.PrefetchScalarGridSpec(
            num_scalar_prefetch=1,
            grid=(bsz, n_chunk),
            in_specs=[pl.BlockSpec((1, N_KV, 8, HEAD_DIM), lambda b, j, pt: (b, 0, 0, 0)),
                      full(w1), full(pe), full(b1), full(w2), full(cos), full(sin), full(mmat)]
            + page_specs,
            out_specs=[pl.BlockSpec((1, N_KV, 8, HEAD_DIM), lambda b, j, pt: (b, 0, 0, 0)),
                       pl.BlockSpec((1, N_KV, 8, HEAD_DIM), lambda b, j, pt: (b, 0, 0, 0))],
            scratch_shapes=[pltpu.VMEM((2 * N_KV, nseg, 2 * CMP_HIDDEN), F32)]),
        compiler_params=_cparams(2),
        name="cmp_sample",
    )(page_table.reshape(-1), q4, w1, pe, b1, w2, cos, sin, mmat,
      *([cache] * len(page_specs)))


def _sel_sample_kernel(idx_ref, pt_ref, q_ref, new_ref, *refs, n_sel, last_blk):
    o_ref = refs[N_KV * n_sel]
    b = pl.program_id(0)
    n_col = 2 * N_KV
    n_key = n_sel * SEL_BLOCK
    slot = lax.broadcasted_iota(jnp.int32, (8, n_key), 1) // SEL_BLOCK
    new = new_ref[0]
    for g in range(N_KV):
        blks = refs[g * n_sel:(g + 1) * n_sel]
        q = q_ref[0, g]
        keys = jnp.concatenate(
            [r[0, pl.ds(g, SEL_BLOCK, stride=n_col), :] for r in blks], axis=0).astype(BF)
        vals = jnp.concatenate(
            [r[0, pl.ds(N_KV + g, SEL_BLOCK, stride=n_col), :] for r in blks], axis=0).astype(BF)
        s = _dot_nt(q, keys) * SCALE
        valid = jnp.zeros((8, n_key), jnp.bool_)
        has_new = jnp.zeros((8, 1), jnp.bool_)
        for kk in range(n_sel):
            bi = idx_ref[(b * N_KV + g) * n_sel + kk]
            valid = valid | ((slot == kk) & (bi < last_blk))
            has_new = has_new | (bi == last_blk)
        k_new = new[g:g + 1].astype(BF).astype(F32)
        v_new = new[N_KV + g:N_KV + g + 1].astype(BF).astype(F32)
        s_new = jnp.sum(q.astype(F32) * k_new, axis=-1, keepdims=True) * SCALE
        s = jnp.where(valid, s, NEG)
        s_new = jnp.where(has_new, s_new, NEG)
        m = jnp.maximum(jnp.max(s, axis=-1, keepdims=True), s_new)
        m = jnp.where(m > 0.5 * NEG, m, 0.0)
        p = jnp.where(valid, jnp.exp(s - m), 0.0)
        p_new = jnp.where(has_new, jnp.exp(s_new - m), 0.0)
        den = jnp.maximum(jnp.sum(p, axis=-1, keepdims=True) + p_new, 1e-30)
        p = p / den
        p_new = p_new / den
        o_ref[0, g] = _dot(p.astype(BF), vals) + p_new.astype(BF).astype(F32) * v_new


def _sel_sample(q_nsa, idx, new_sel, cache_sel, page_table):
    bsz, n_pages = page_table.shape
    n_phys = cache_sel.shape[0]
    n_sel = idx.shape[2]
    halves = PAGE // SEL_BLOCK
    last_blk = n_pages * halves
    blk_rows = SEL_BLOCK * 2 * N_KV
    cache = cache_sel.reshape(n_phys * halves, blk_rows, HEAD_DIM)
    q4 = q_nsa.reshape(bsz, N_KV, GROUP_R, HEAD_DIM)
    q4 = jnp.pad(q4, ((0, 0), (0, 0), (0, 8 - GROUP_R), (0, 0)))
    new4 = new_sel.reshape(bsz, 2 * N_KV, HEAD_DIM)

    def blk_spec(g, kk):
        def imap(b, idx_r, pt_r):
            bi = jnp.minimum(idx_r[(b * N_KV + g) * n_sel + kk], last_blk - 1)
            return (pt_r[b * n_pages + bi // halves] * halves + bi % halves, 0, 0)
        return pl.BlockSpec((1, blk_rows, HEAD_DIM), imap)

    return pl.pallas_call(
        functools.partial(_sel_sample_kernel, n_sel=n_sel, last_blk=last_blk),
        out_shape=jax.ShapeDtypeStruct((bsz, N_KV, 8, HEAD_DIM), F32),
        grid_spec=pltpu.PrefetchScalarGridSpec(
            num_scalar_prefetch=2,
            grid=(bsz,),
            in_specs=[pl.BlockSpec((1, N_KV, 8, HEAD_DIM), lambda b, i_r, p_r: (b, 0, 0, 0)),
                      pl.BlockSpec((1, 2 * N_KV, HEAD_DIM), lambda b, i_r, p_r: (b, 0, 0))]
            + [blk_spec(g, kk) for g in range(N_KV) for kk in range(n_sel)],
            out_specs=pl.BlockSpec((1, N_KV, 8, HEAD_DIM), lambda b, i_r, p_r: (b, 0, 0, 0))),
        compiler_params=_cparams(1),
        name="sel_sample",
    )(idx.reshape(-1), page_table.reshape(-1), q4, new4, *([cache] * (N_KV * n_sel)))


def _win_sample_kernel(q_ref, win_ref, new_ref, oc_ref, os_ref, g_ref, o_ref, nw_ref, *, w_buf):
    n_col = 2 * N_KV
    new = new_ref[0]
    i = lax.broadcasted_iota(jnp.int32, (8, w_buf), 1)
    valid = (w_buf - i >= 0) & (w_buf - i < WINDOW)
    gate = jax.nn.sigmoid(g_ref[0])
    lane_g = lax.broadcasted_iota(jnp.int32, gate.shape, 1)
    row8 = lax.broadcasted_iota(jnp.int32, (8, HEAD_DIM), 0)
    outs = []
    for g in range(N_KV):
        q = q_ref[0, g]
        k = win_ref[0, pl.ds(g, w_buf, stride=n_col), :].astype(BF)
        v = win_ref[0, pl.ds(N_KV + g, w_buf, stride=n_col), :].astype(BF)
        k_new = new[g:g + 1].astype(BF).astype(F32)
        v_new = new[N_KV + g:N_KV + g + 1].astype(BF).astype(F32)
        s = jnp.where(valid, _dot_nt(q, k) * SCALE, NEG)
        s_new = jnp.sum(q.astype(F32) * k_new, axis=-1, keepdims=True) * SCALE
        m = jnp.maximum(jnp.max(s, axis=-1, keepdims=True), s_new)
        p = jnp.where(valid, jnp.exp(s - m), 0.0)
        p_new = jnp.exp(s_new - m)
        den = jnp.sum(p, axis=-1, keepdims=True) + p_new
        p = p / den
        p_new = p_new / den
        o_w = _dot(p.astype(BF), v) + p_new.astype(BF).astype(F32) * v_new
        o_c, o_s = oc_ref[0, g], os_ref[0, g]
        comb = jnp.zeros((8, HEAD_DIM), F32)
        for h in range(GROUP_R):
            base = (g * GROUP_R + h) * 3
            g3 = [jnp.sum(jnp.where(lane_g == base + bb, gate, 0.0), axis=-1, keepdims=True)
                  for bb in range(3)]
            mix = g3[0] * o_c + g3[1] * o_s + g3[2] * o_w
            comb = jnp.where(row8 == h, mix, comb)
        outs.append(comb)
    o_ref[0] = jnp.concatenate(outs, axis=0).astype(o_ref.dtype)
    n_rows = w_buf * n_col
    nw_ref[0] = pltpu.roll(win_ref[0], n_rows - n_col, 0)
    nw_ref[0, n_rows - n_col:, :] = new


def _win_sample(q_nsa, cache_win, new_win, o_c, o_s, g_nsa):
    bsz, w_buf = cache_win.shape[0], cache_win.shape[1]
    n_col = 2 * N_KV
    q4 = q_nsa.reshape(bsz, N_KV, GROUP_R, HEAD_DIM)
    q4 = jnp.pad(q4, ((0, 0), (0, 0), (0, 8 - GROUP_R), (0, 0)))
    o, nw = pl.pallas_call(
        functools.partial(_win_sample_kernel, w_buf=w_buf),
        out_shape=[jax.ShapeDtypeStruct((bsz, 16, HEAD_DIM), BF),
                   jax.ShapeDtypeStruct((bsz, w_buf * n_col, HEAD_DIM), F32)],
        grid=(bsz,),
        in_specs=[pl.BlockSpec((1, N_KV, 8, HEAD_DIM), lambda b: (b, 0, 0, 0)),
                  pl.BlockSpec((1, w_buf * n_col, HEAD_DIM), lambda b: (b, 0, 0)),
                  pl.BlockSpec((1, n_col, HEAD_DIM), lambda b: (b, 0, 0)),
                  pl.BlockSpec((1, N_KV, 8, HEAD_DIM), lambda b: (b, 0, 0, 0)),
                  pl.BlockSpec((1, N_KV, 8, HEAD_DIM), lambda b: (b, 0, 0, 0)),
                  pl.BlockSpec((1, 1, HEAD_DIM), lambda b: (b, 0, 0))],
        out_specs=[pl.BlockSpec((1, 16, HEAD_DIM), lambda b: (b, 0, 0)),
                   pl.BlockSpec((1, w_buf * n_col, HEAD_DIM), lambda b: (b, 0, 0))],
        compiler_params=_cparams(1),
        name="win_sample",
    )(q4, cache_win.reshape(bsz, w_buf * n_col, HEAD_DIM), new_win.reshape(bsz, n_col, HEAD_DIM),
      o_c, o_s, g_nsa.reshape(bsz, 1, HEAD_DIM))
    o = o.reshape(bsz, N_KV, 8, HEAD_DIM)[:, :, :GROUP_R].reshape(bsz, N_H_NSA * HEAD_DIM)
    return o, nw


def _branch_kernel(on_ref, os_ref, wn_ref, ws_ref, mn_ref, ms_ref, o_ref):
    y = (jax.nn.sigmoid(mn_ref[...]) * _dot(on_ref[...], wn_ref[...])
         + jax.nn.sigmoid(ms_ref[...]) * _dot(os_ref[...], ws_ref[...]))
    o_ref[...] = y.astype(o_ref.dtype)


def _branch_mix(o_nsa, o_sb, w_br_nsa, w_br_sb, m_all, tm):
    m, k = o_nsa.shape
    d = w_br_nsa.shape[1]
    tn = 1024
    nj = d // tn
    return pl.pallas_call(
        _branch_kernel,
        out_shape=jax.ShapeDtypeStruct((m, d), BF),
        grid=(nj, m // tm),
        in_specs=[pl.BlockSpec((tm, k), lambda j, i: (i, 0)),
                  pl.BlockSpec((tm, k), lambda j, i: (i, 0)),
                  pl.BlockSpec((k, tn), lambda j, i: (0, j)),
                  pl.BlockSpec((k, tn), lambda j, i: (0, j)),
                  pl.BlockSpec((tm, tn), lambda j, i: (i, j)),
                  pl.BlockSpec((tm, tn), lambda j, i: (i, nj + j))],
        out_specs=pl.BlockSpec((tm, tn), lambda j, i: (i, j)),
        compiler_params=_cparams(2),
        name="branch_mix",
    )(o_nsa, o_sb, w_br_nsa, w_br_sb, m_all, m_all)


def _rms(x):
    return x * lax.rsqrt(jnp.mean(x * x, axis=-1, keepdims=True) + EPS)


def _post_mix_kernel(y_ref, w_ref, x_ref, gate_ref, gpost_ref, gpre_ref, sc_ref, sh_ref,
                     x1_ref, h_ref):
    y2 = _dot(y_ref[...], w_ref[...])
    x1 = x_ref[...] + _rows(gate_ref) * (_rms(y2) * gpost_ref[...])
    x1_ref[...] = x1
    h_ref[...] = (_rms(x1) * gpre_ref[...] * (1.0 + _rows(sc_ref))
                  + _rows(sh_ref)).astype(h_ref.dtype)


def _mod_spec2(tm, d, k, row_of_tile):
    if row_of_tile is None:
        return pl.BlockSpec((tm, d), lambda i: (i, k))
    return pl.BlockSpec((1, 1, d), lambda i: (row_of_tile(i), 0, k))


def _post_mix(y, w_out, x2d, mod, g_post_mix, g_pre_ffn, tm, row_of_tile):
    m, d = x2d.shape
    vec = lambda: pl.BlockSpec((1, d), lambda i: (0, 0))
    return pl.pallas_call(
        _post_mix_kernel,
        out_shape=[jax.ShapeDtypeStruct((m, d), F32), jax.ShapeDtypeStruct((m, d), BF)],
        grid=(m // tm,),
        in_specs=[pl.BlockSpec((tm, d), lambda i: (i, 0)),
                  pl.BlockSpec((d, d), lambda i: (0, 0)),
                  pl.BlockSpec((tm, d), lambda i: (i, 0)),
                  _mod_spec2(tm, d, 2, row_of_tile), vec(), vec(),
                  _mod_spec2(tm, d, 4, row_of_tile), _mod_spec2(tm, d, 3, row_of_tile)],
        out_specs=[pl.BlockSpec((tm, d), lambda i: (i, 0)),
                   pl.BlockSpec((tm, d), lambda i: (i, 0))],
        compiler_params=_cparams(1),
        name="post_mix",
    )(y, w_out, x2d, mod, g_post_mix.reshape(1, d), g_pre_ffn.reshape(1, d), mod, mod)


def _ffn_kernel(h_ref, wg_ref, wu_ref, wd_ref, x1_ref, gate_ref, gpost_ref, o_ref, acc_ref):
    j = pl.program_id(1)

    @pl.when(j == 0)
    def _():
        acc_ref[...] = jnp.zeros_like(acc_ref)

    h = h_ref[...]
    a = _dot(h, wg_ref[...])
    act = (a * jax.nn.sigmoid(a) * _dot(h, wu_ref[...])).astype(BF)
    acc_ref[...] += _dot(act, wd_ref[...])

    @pl.when(j == pl.num_programs(1) - 1)
    def _():
        o_ref[...] = x1_ref[...] + _rows(gate_ref) * (_rms(acc_ref[...]) * gpost_ref[...])


def _ffn(h2, wg, wu, wd, x1, mod, g_post_ffn, tm, row_of_tile):
    m, d = x1.shape
    dff = wg.shape[1]
    tf = 512
    if row_of_tile is None:
        gate_spec = pl.BlockSpec((tm, d), lambda i, j: (i, 5))
    else:
        gate_spec = pl.BlockSpec((1, 1, d), lambda i, j: (row_of_tile(i), 0, 5))
    return pl.pallas_call(
        _ffn_kernel,
        out_shape=jax.ShapeDtypeStruct((m, d), F32),
        grid=(m // tm, dff // tf),
        in_specs=[pl.BlockSpec((tm, d), lambda i, j: (i, 0)),
                  pl.BlockSpec((d, tf), lambda i, j: (0, j)),
                  pl.BlockSpec((d, tf), lambda i, j: (0, j)),
                  pl.BlockSpec((tf, d), lambda i, j: (j, 0)),
                  pl.BlockSpec((tm, d), lambda i, j: (i, 0)),
                  gate_spec,
                  pl.BlockSpec((1, d), lambda i, j: (0, 0))],
        out_specs=pl.BlockSpec((tm, d), lambda i, j: (i, 0)),
        scratch_shapes=[pltpu.VMEM((tm, d), F32)],
        compiler_params=_cparams(2),
        name="ffn",
    )(h2, wg, wu, wd, x1, mod, g_post_ffn.reshape(1, d))


def _split_w_in(w_in, d_model):
    q = N_H_NSA * HEAD_DIM
    kv = N_KV * HEAD_DIM
    sb = N_H_SB * HEAD_DIM
    gn = N_H_NSA * 3
    o = 0
    cols = {}
    for name, width in (("q_nsa", q), ("cmp", 2 * kv), ("sel", 2 * kv), ("win", 2 * kv),
                        ("g", gn), ("q_sb", sb), ("kv_sb", 2 * sb), ("m", 2 * d_model)):
        cols[name] = w_in[:, o:o + width].astype(BF)
        o += width
    cols["g"] = jnp.pad(cols["g"], ((0, 0), (0, HEAD_DIM - gn)))
    return cols


def _group(x2d, mod, row_of_tile, tm, pos, w, tail_w):
    g_pre_mix = tail_w["g_pre_mix"]
    h = _prenorm(x2d, g_pre_mix, mod, 1, 0, tm, row_of_tile)
    cos, sin = _rope_tables(pos)
    proj = {}
    proj["q_nsa"], = _project(h, w["q_nsa"], cos, sin, tm, N_H_NSA, (BF,))
    proj["cmp"], = _project(h, w["cmp"], cos, sin, tm, 0, (), cache_rows=True)
    proj["sel"], proj["sel_bf"] = _project(h, w["sel"], cos, sin, tm, N_KV, (BF,), cache_rows=True)
    proj["win"], proj["win_bf"] = _project(h, w["win"], cos, sin, tm, N_KV, (BF,), cache_rows=True)
    proj["g"], = _project(h, w["g"], cos, sin, tm, 0, (F32,))
    proj["q_sb"], = _project(h, w["q_sb"], cos, sin, tm, 0, (BF,))
    proj["kv_sb"], proj["kv_sb_bf"] = _project(h, w["kv_sb"], cos, sin, tm, 0, (BF,),
                                               cache_rows=True)
    proj["m"], = _project(h, w["m"], cos, sin, tm, 0, (F32,))
    return proj


def _tail(x2d, o_nsa, o_sb, m_all, mod, row_of_tile, tm, tw):
    y = _branch_mix(o_nsa, o_sb, tw["w_br_nsa"], tw["w_br_sb"], m_all, tm)
    x1, h2 = _post_mix(y, tw["w_out"], x2d, mod, tw["g_post_mix"], tw["g_pre_ffn"], tm,
                       row_of_tile)
    return _ffn(h2, tw["w_ffn_gate"], tw["w_ffn_up"], tw["w_ffn_down"], x1, mod,
                tw["g_post_ffn"], tm, row_of_tile)


def kernel(x_prompt, x_sample, c_prompt, c_sample, cache_cmp, cache_sel, cache_sb, cache_win,
           page_table, w_ada, b_ada, g_pre_mix, g_post_mix, g_pre_ffn, g_post_ffn, w_in,
           pe_ck, w_ck1, b_ck1, w_ck2, pe_cv, w_cv1, b_cv1, w_cv2,
           w_br_nsa, w_br_sb, w_out, w_ffn_gate, w_ffn_up, w_ffn_down):
    n_p, t_p, d = x_prompt.shape
    n_s, t_s, _ = x_sample.shape
    assert t_s == 1
    n_pages = page_table.shape[1]
    past = n_pages * PAGE
    w_buf = cache_win.shape[1]
    kvw = 2 * N_KV * HEAD_DIM

    w = _split_w_in(w_in, d)
    tw = dict(g_pre_mix=g_pre_mix, g_post_mix=g_post_mix, g_pre_ffn=g_pre_ffn,
              g_post_ffn=g_post_ffn, w_br_nsa=w_br_nsa.astype(BF), w_br_sb=w_br_sb.astype(BF),
              w_out=w_out.astype(BF), w_ffn_gate=w_ffn_gate.astype(BF),
              w_ffn_up=w_ffn_up.astype(BF), w_ffn_down=w_ffn_down.astype(BF))
    cmp_w = _cmp_weights(pe_ck, w_ck1, b_ck1, w_ck2, pe_cv, w_cv1, b_cv1, w_cv2)

    n_mod = -(-(n_s + n_p) // 16) * 16
    c_all = jnp.concatenate([c_sample, c_prompt, jnp.zeros((n_mod - n_s - n_p, d), F32)])
    mod = _adaln(c_all, w_ada, b_ada)
    mod3 = mod.reshape(n_mod, 1, 6 * d)

    tm_p = 512
    tiles_per_seq = t_p // tm_p
    row_p = lambda i: n_s + i // tiles_per_seq
    xp = x_prompt.reshape(n_p * t_p, d)
    pp = _group(xp, mod3, row_p, tm_p, jnp.arange(t_p), w, tw)
    kvc = _compress_prompt(pp["cmp"], cmp_w, n_p, t_p)
    tq = 256
    o_nsa = _nsa_prompt(pp["q_nsa"], kvc, pp["sel_bf"], pp["win_bf"], pp["g"], n_p, t_p, tq)
    o_sb = _sb_prompt(pp["q_sb"], pp["kv_sb_bf"], n_p, t_p, tq)
    y_prompt = _tail(xp, o_nsa, o_sb, pp["m"], mod3, row_p, tm_p, tw).reshape(n_p, t_p, d)
    new_cmp_p = pp["cmp"].reshape(n_p, t_p, 2, N_KV, HEAD_DIM)
    new_sel_p = pp["sel"].reshape(n_p, t_p, 2, N_KV, HEAD_DIM)
    new_sb_p = pp["kv_sb"].reshape(n_p, t_p, 2, N_H_SB, HEAD_DIM)
    win_rows = pp["win"].reshape(n_p, t_p, 2, N_KV, HEAD_DIM)
    if t_p >= w_buf:
        new_win_p = win_rows[:, t_p - w_buf:]
    else:
        new_win_p = jnp.pad(win_rows, ((0, 0), (w_buf - t_p, 0), (0, 0), (0, 0), (0, 0)))

    xs = x_sample.reshape(n_s, d)
    pos_s = jnp.full((n_s,), past, jnp.int32)
    ps = _group(xs, mod, None, n_s, pos_s, w, tw)
    o_c, idx = _cmp_sample(ps["q_nsa"], cache_cmp, page_table, cmp_w, past)
    idx = idx[:, :, 0, :min(N_SELECT, -(-(past + 1) // SEL_BLOCK))]
    o_s = _sel_sample(ps["q_nsa"], idx, ps["sel"], cache_sel, page_table)
    o_nsa_s, new_win_s = _win_sample(ps["q_nsa"], cache_win, ps["win"], o_c, o_s, ps["g"])
    o_sb_s = _sb_sample(ps["q_sb"], cache_sb, page_table)
    y_sample = _tail(xs, o_nsa_s, o_sb_s, ps["m"], mod, None, n_s, tw).reshape(n_s, 1, d)

    return (y_prompt, y_sample, new_cmp_p, new_sel_p, new_sb_p, new_win_p,
            ps["cmp"].reshape(n_s, 1, 2, N_KV, HEAD_DIM),
            ps["sel"].reshape(n_s, 1, 2, N_KV, HEAD_DIM),
            ps["kv_sb"].reshape(n_s, 1, 2, N_H_SB, HEAD_DIM),
            new_win_s.reshape(n_s, w_buf, 2, N_KV, HEAD_DIM))
```
